```python
import math
import jax, jax.numpy as jnp
from jax import lax
import numpy as np

D_MODEL = 1024
BATCH = 32
SEQ = 256
DEPTH = 2
DEC_BATCH = 4
DEC_SEQ = 4096
PAST_LEN = 256

GRID_W = 64
N_MIXERS = 2
N_DA = (DEPTH + 1) // 2
N_ML = DEPTH // 2
DA_HEADS = 8
DA_DK = 64
DA_DV = 2 * DA_DK
DA_QK = DA_HEADS * 2 * DA_DK
DA_V = DA_HEADS * DA_DV
ML_HEADS = 4
ML_DK = D_MODEL // 2 // ML_HEADS
ML_DV = D_MODEL // ML_HEADS
ML_QK = ML_HEADS * ML_DK
ML_V = ML_HEADS * ML_DV
ML_GATES = 2 * 2 * ML_HEADS
ML_IN = 2 * ML_QK + 2 * ML_V + ML_GATES
ML_CHUNK = 64
CONV_W = 3
D_FF = 2816
Q_BLOCK = 128
ROPE_THETA = 10000.0
ALPHA = (2 * DEPTH) ** 0.25
BETA = (8 * DEPTH) ** -0.25
EPS = 1e-5

kernel_name = 'hybrid_diffattn_mlstm_dit_step'


def layer_norm(x, g, b):
    xf = x.astype(jnp.float32)
    mu = jnp.mean(xf, axis=-1, keepdims=True)
    var = jnp.mean(jnp.square(xf - mu), axis=-1, keepdims=True)
    return ((xf - mu) * lax.rsqrt(var + EPS) * g + b).astype(x.dtype)


def post_norm(x, y, g, b):
    return layer_norm(ALPHA * x + y, g, b)


def modulate(x, shift, scale):
    return x * (1 + scale) + shift


def dwconv3(x, w, b):
    L = x.shape[1]
    xp = jnp.pad(x, ((0, 0), (1, 1), (0, 0)))
    return xp[:, :L] * w[0] + xp[:, 1:L + 1] * w[1] + xp[:, 2:] * w[2] + b


def conv_ffn(h, w_up, conv_w, conv_b, w_down):
    u = dwconv3(h @ w_up, conv_w, conv_b)
    g, val = jnp.split(u, 2, axis=-1)
    return (jax.nn.silu(g) * val) @ w_down


def rope_2d(x):
    L = x.shape[1]
    rows = L // GRID_W
    row = jnp.repeat(jnp.arange(rows), GRID_W).astype(jnp.float32)
    col = jnp.tile(jnp.arange(GRID_W), rows).astype(jnp.float32)
    half = DA_DK // 2
    nf = half // 2
    inv = ROPE_THETA ** (-jnp.arange(nf, dtype=jnp.float32) / nf)

    def rot(xa, pos):
        ang = pos[:, None] * inv[None, :]
        cos = jnp.cos(ang)[None, :, None, None, :]
        sin = jnp.sin(ang)[None, :, None, None, :]
        x1, x2 = xa[..., :nf], xa[..., nf:]
        return jnp.concatenate([x1 * cos - x2 * sin, x2 * cos + x1 * sin], axis=-1)

    xf = x.astype(jnp.float32)
    return jnp.concatenate([rot(xf[..., :half], row), rot(xf[..., half:], col)], axis=-1).astype(x.dtype)


def da_project(h, w_qkv):
    B, L, _ = h.shape
    q, k, v = jnp.split(h @ w_qkv, [DA_QK, 2 * DA_QK], axis=-1)
    return (q.reshape(B, L, DA_HEADS, 2, DA_DK), k.reshape(B, L, DA_HEADS, 2, DA_DK),
            v.reshape(B, L, DA_HEADS, DA_DV))


def da_lambda(lam, layer):
    lam_init = 0.8 - 0.6 * math.exp(-0.3 * layer)
    lf = lam.astype(jnp.float32)
    lam_full = jnp.exp(jnp.sum(lf[0] * lf[1])) - jnp.exp(jnp.sum(lf[2] * lf[3])) + lam_init
    return lam_full, lam_init


def diff_attend(q, k, v, lam):
    B, Lq = q.shape[:2]
    nb = Lq // Q_BLOCK
    qb = jnp.moveaxis(q.reshape(B, nb, Q_BLOCK, DA_HEADS, 2, DA_DK), 1, 0)
    scale = DA_DK ** -0.5

    def block(qi):
        s = jnp.einsum('bqhmd,bkhmd->bhmqk', qi, k).astype(jnp.float32) * scale
        p = jax.nn.softmax(s, axis=-1)
        a = (p[:, :, 0] - lam * p[:, :, 1]).astype(v.dtype)
        return jnp.einsum('bhqk,bkhd->bqhd', a, v)

    o = lax.map(block, qb)
    return jnp.moveaxis(o, 0, 1).reshape(B, Lq, DA_HEADS, DA_DV)


def da_output(o, subln_w, lam_init, w_o):
    B, L = o.shape[:2]
    of = o.astype(jnp.float32)
    of = of * lax.rsqrt(jnp.mean(jnp.square(of), axis=-1, keepdims=True) + EPS) * subln_w
    of = of * (1.0 - lam_init)
    return of.astype(o.dtype).reshape(B, L, DA_V) @ w_o


def ml_project(h, w_in, conv_w, conv_b, b_gate):
    B, L, _ = h.shape
    qk, v, o, gates = jnp.split(h @ w_in, [2 * ML_QK, 2 * ML_QK + ML_V, 2 * ML_QK + 2 * ML_V], axis=-1)
    qk = jax.nn.silu(dwconv3(qk, conv_w, conv_b))
    q, k = jnp.split(qk, 2, axis=-1)
    q = q.reshape(B, L, ML_HEADS, ML_DK).transpose(0, 2, 1, 3)
    k = k.reshape(B, L, ML_HEADS, ML_DK).transpose(0, 2, 1, 3) * (ML_DK ** -0.5)
    v = v.reshape(B, L, ML_HEADS, ML_DV).transpose(0, 2, 1, 3)
    gates = (gates + b_gate).astype(jnp.float32).reshape(B, L, 2, 2, ML_HEADS)
    gates = jnp.transpose(gates, (2, 3, 0, 4, 1))
    ig = gates[:, 0]
    lf = jax.nn.log_sigmoid(gates[:, 1])
    return q, k, v, o, ig, lf


def mlstm_scan(q, k, v, ig, lf, C0, n0, m0):
    B, H, L, _ = q.shape
    nc = L // ML_CHUNK

    def chunks(x):
        x = x.astype(jnp.float32).reshape((B, H, nc, ML_CHUNK) + x.shape[3:])
        return jnp.moveaxis(x, 2, 0)

    tril = jnp.tril(jnp.ones((ML_CHUNK, ML_CHUNK), dtype=bool))

    def step(carry, inp):
        C, n, m = carry
        qc, kc, vc, ic, fc = inp
        b = jnp.cumsum(fc, axis=-1)
        intra = jnp.where(tril, b[..., :, None] - b[..., None, :] + ic[..., None, :], -jnp.inf)
        inter = b + m[..., None]
        m_row = jnp.maximum(inter, jnp.max(intra, axis=-1))
        w_intra = jnp.exp(intra - m_row[..., None])
        w_inter = jnp.exp(inter - m_row)
        s = jnp.einsum('bhtd,bhsd->bhts', qc, kc) * w_intra
        num = w_inter[..., None] * jnp.einsum('bhtd,bhde->bhte', qc, C) + jnp.einsum('bhts,bhse->bhte', s, vc)
        den = w_inter * jnp.einsum('bhtd,bhd->bht', qc, n) + jnp.sum(s, axis=-1)
        h = num / jnp.maximum(jnp.abs(den), jnp.exp(-m_row))[..., None]
        b_last = b[..., -1]
        dec = b_last[..., None] - b + ic
        m_new = jnp.maximum(b_last + m, jnp.max(dec, axis=-1))
        w_s = jnp.exp(dec - m_new[..., None])
        carry_w = jnp.exp(b_last + m - m_new)
        C_new = carry_w[..., None, None] * C + jnp.einsum('bhs,bhsd,bhse->bhde', w_s, kc, vc)
        n_new = carry_w[..., None] * n + jnp.einsum('bhs,bhsd->bhd', w_s, kc)
        return (C_new, n_new, m_new), h

    carry0 = (C0.astype(jnp.float32), n0.astype(jnp.float32), m0.astype(jnp.float32))
    (C, n, m), h = lax.scan(step, carry0, (chunks(q), chunks(k), chunks(v), chunks(ig), chunks(lf)))
    h = jnp.moveaxis(h, 0, 2).reshape(B, H, L, ML_DV)
    return h, C, n, m


def ml_bidir(q, k, v, ig, lf, C0, n0, m0):
    hf, Cf, nf, mf = mlstm_scan(q, k, v, ig[0], lf[0], C0[:, 0], n0[:, 0], m0[:, 0])
    rev = lambda x: jnp.flip(x, axis=2)
    hb, Cb, nb, mb = mlstm_scan(rev(q), rev(k), rev(v), rev(ig[1]), rev(lf[1]), C0[:, 1], n0[:, 1], m0[:, 1])
    h = hf + rev(hb)
    return h, jnp.stack([Cf, Cb], axis=1), jnp.stack([nf, nb], axis=1), jnp.stack([mf, mb], axis=1)


def ml_output(h, o, norm_w, w_out):
    B, H, L, _ = h.shape
    mu = jnp.mean(h, axis=-1, keepdims=True)
    var = jnp.mean(jnp.square(h - mu), axis=-1, keepdims=True)
    hn = (h - mu) * lax.rsqrt(var + EPS) * norm_w.astype(jnp.float32)[:, None, :]
    hn = hn.transpose(0, 2, 1, 3).reshape(B, L, ML_V).astype(o.dtype)
    return (jax.nn.sigmoid(o) * hn) @ w_out


def setup_inputs(seed: int = 0) -> dict:
    key = jax.random.key(seed)
    ks = jax.random.split(key, 32)
    nrm = lambda k, shape, s: jax.random.normal(k, shape, jnp.float32) * s
    ada_b = jnp.concatenate([nrm(ks[6], (DEPTH, 2, 2 * D_MODEL), 0.02),
                             1.0 + nrm(ks[7], (DEPTH, 2, D_MODEL), 0.02)], axis=-1).reshape(DEPTH, 6 * D_MODEL)
    ib = nrm(ks[16], (N_ML, 2, 1, ML_HEADS), 0.1)
    fb = jnp.linspace(3.0, 6.0, ML_HEADS, dtype=jnp.float32)[None, None, None, :] + nrm(ks[17], (N_ML, 2, 1, ML_HEADS), 0.1)
    ml_b_gate = jnp.concatenate([ib, fb], axis=2).reshape(N_ML, ML_GATES)
    return {
        'x_prompt': nrm(ks[0], (BATCH, SEQ, D_MODEL), 1.0),
        'x_sample': nrm(ks[1], (DEC_BATCH, DEC_SEQ, D_MODEL), 1.0),
        'c': nrm(ks[2], (DEC_BATCH, D_MODEL), 1.0),
        'cache_k': nrm(ks[3], (DEC_BATCH, N_DA, PAST_LEN, DA_HEADS, 2, DA_DK), 1.0),
        'cache_v': nrm(ks[4], (DEC_BATCH, N_DA, PAST_LEN, DA_HEADS, DA_DV), 1.0),
        'state_C': nrm(ks[24], (DEC_BATCH, N_ML, 2, ML_HEADS, ML_DK, ML_DV), 0.5),
        'state_n': nrm(ks[25], (DEC_BATCH, N_ML, 2, ML_HEADS, ML_DK), 0.5),
        'state_m': nrm(ks[26], (DEC_BATCH, N_ML, 2, ML_HEADS), 1.0),
        'c_ctx': nrm(ks[5], (D_MODEL,), 1.0),
        'ada_w': nrm(ks[8], (DEPTH, D_MODEL, 6 * D_MODEL), 0.02),
        'ada_b': ada_b,
        'ln_g': 1.0 + nrm(ks[9], (DEPTH, 2, D_MODEL), 0.02),
        'ln_b': nrm(ks[10], (DEPTH, 2, D_MODEL), 0.02),
        'da_w_qkv': nrm(ks[11], (N_DA, D_MODEL, 2 * DA_QK + DA_V), D_MODEL ** -0.5),
        'da_lam': nrm(ks[12], (N_DA, 4, DA_DK), 0.1),
        'da_subln': 1.0 + nrm(ks[13], (N_DA, DA_DV), 0.02),
        'da_w_o': nrm(ks[14], (N_DA, DA_V, D_MODEL), BETA * DA_V ** -0.5),
        'ml_w_in': nrm(ks[15], (N_ML, D_MODEL, ML_IN), D_MODEL ** -0.5),
        'ml_conv_w': nrm(ks[18], (N_ML, CONV_W, 2 * ML_QK), 0.5),
        'ml_conv_b': nrm(ks[19], (N_ML, 2 * ML_QK), 0.02),
        'ml_b_gate': ml_b_gate,
        'ml_norm_w': 1.0 + nrm(ks[20], (N_ML, ML_HEADS, ML_DV), 0.02),
        'ml_w_out': nrm(ks[21], (N_ML, ML_V, D_MODEL), BETA * ML_V ** -0.5),
        'ffn_w_up': nrm(ks[22], (DEPTH, D_MODEL, 2 * D_FF), D_MODEL ** -0.5),
        'ffn_conv_w': nrm(ks[23], (DEPTH, CONV_W, 2 * D_FF), 0.5),
        'ffn_conv_b': nrm(ks[27], (DEPTH, 2 * D_FF), 0.02),
        'ffn_w_down': nrm(ks[28], (DEPTH, D_FF, D_MODEL), BETA * D_FF ** -0.5),
    }


def reference(x_prompt, x_sample, c, cache_k, cache_v, state_C, state_n, state_m, c_ctx,
              ada_w, ada_b, ln_g, ln_b, da_w_qkv, da_lam, da_subln, da_w_o,
              ml_w_in, ml_conv_w, ml_conv_b, ml_b_gate, ml_norm_w, ml_w_out,
              ffn_w_up, ffn_conv_w, ffn_conv_b, ffn_w_down):
    cond_p = jax.nn.silu(c_ctx)
    cond_s = jax.nn.silu(c)
    xp, xs = x_prompt, x_sample
    bp = x_prompt.shape[0]
    new_k, new_v, new_C, new_n, new_m = [], [], [], [], []
    for i in range(DEPTH):
        j = i // N_MIXERS
        mod_p = jnp.split(cond_p @ ada_w[i] + ada_b[i], 6, axis=-1)
        mod_s = jnp.split((cond_s @ ada_w[i] + ada_b[i])[:, None, :], 6, axis=-1)
        hp = modulate(xp, mod_p[0], mod_p[1])
        hs = modulate(xs, mod_s[0], mod_s[1])
        if i % N_MIXERS == 0:
            lam, lam_init = da_lambda(da_lam[j], i)
            qp, kp, vp = da_project(hp, da_w_qkv[j])
            yp = da_output(diff_attend(qp, kp, vp, lam), da_subln[j], lam_init, da_w_o[j])
            new_k.append(kp)
            new_v.append(vp)
            qs, ks_, vs = da_project(hs, da_w_qkv[j])
            k_all = jnp.concatenate([cache_k[:, j], rope_2d(ks_)], axis=1)
            v_all = jnp.concatenate([cache_v[:, j], vs], axis=1)
            ys = da_output(diff_attend(rope_2d(qs), k_all, v_all, lam), da_subln[j], lam_init, da_w_o[j])
        else:
            qp, kp, vp, op, igp, lfp = ml_project(hp, ml_w_in[j], ml_conv_w[j], ml_conv_b[j], ml_b_gate[j])
            zC = jnp.zeros((bp, 2, ML_HEADS, ML_DK, ML_DV), jnp.float32)
            zn = jnp.zeros((bp, 2, ML_HEADS, ML_DK), jnp.float32)
            zm = jnp.zeros((bp, 2, ML_HEADS), jnp.float32)
            hpm, Cp, npn, mpm = ml_bidir(qp, kp, vp, igp, lfp, zC, zn, zm)
            yp = ml_output(hpm, op, ml_norm_w[j], ml_w_out[j])
            new_C.append(Cp)
            new_n.append(npn)
            new_m.append(mpm)
            qs, ks_, vs, os_, igs, lfs = ml_project(hs, ml_w_in[j], ml_conv_w[j], ml_conv_b[j], ml_b_gate[j])
            hsm, _, _, _ = ml_bidir(qs, ks_, vs, igs, lfs, state_C[:, j], state_n[:, j], state_m[:, j])
            ys = ml_output(hsm, os_, ml_norm_w[j], ml_w_out[j])
        xp = post_norm(xp, mod_p[2] * yp, ln_g[i, 0], ln_b[i, 0])
        xs = post_norm(xs, mod_s[2] * ys, ln_g[i, 0], ln_b[i, 0])
        hp = modulate(xp, mod_p[3], mod_p[4])
        hs = modulate(xs, mod_s[3], mod_s[4])
        fp = conv_ffn(hp, ffn_w_up[i], ffn_conv_w[i], ffn_conv_b[i], ffn_w_down[i])
        fs = conv_ffn(hs, ffn_w_up[i], ffn_conv_w[i], ffn_conv_b[i], ffn_w_down[i])
        xp = post_norm(xp, mod_p[5] * fp, ln_g[i, 1], ln_b[i, 1])
        xs = post_norm(xs, mod_s[5] * fs, ln_g[i, 1], ln_b[i, 1])
    return (xp, xs, jnp.stack(new_k, axis=1), jnp.stack(new_v, axis=1), jnp.stack(new_C, axis=1), jnp.stack(new_n, axis=1), jnp.stack(new_m, axis=1))
```

```python
import functools
import math

import jax
import jax.numpy as jnp
from jax import lax
from jax.experimental import pallas as pl
from jax.experimental.pallas import tpu as pltpu

D_MODEL = 1024
DEPTH = 2
GRID_W = 64
DA_HEADS = 8
DA_DK = 64
DA_DV = 128
ML_HEADS = 4
ML_DK = 128
ML_DV = 256
ML_QK = ML_HEADS * ML_DK
ML_V = ML_HEADS * ML_DV
ML_GATES = 16
D_FF = 2816
ROPE_THETA = 10000.0
ALPHA = (2 * DEPTH) ** 0.25
EPS = 1e-5

_BF = jnp.bfloat16
_F32 = jnp.float32
_NEG = -1e30

_VMEM_LIMIT_BYTES = 56 * 1024 * 1024
_HALO = 16
_ML_CHUNK = 256
_FF_CHUNK = 256


def _dot(a, b):
    return jnp.dot(a, b, preferred_element_type=_F32)


def _dot_nt(a, b):
    return lax.dot_general(a, b, (((1,), (1,)), ((), ())), preferred_element_type=_F32)


def _split2(x):
    hi = x.astype(_BF)
    lo = (x - hi.astype(_F32)).astype(_BF)
    return hi, lo


def _split3(x):
    hi = x.astype(_BF)
    r = x - hi.astype(_F32)
    mid = r.astype(_BF)
    lo = (r - mid.astype(_F32)).astype(_BF)
    return hi, mid, lo


def _dot_f32(a, b):
    ah, al = _split2(a)
    bh, bl = _split2(b)
    return _dot(ah, bh) + _dot(al, bh) + _dot(ah, bl)


def _dot_nt_f32(a, b):
    ah, al = _split2(a)
    bh, bl = _split2(b)
    return _dot_nt(ah, bh) + _dot_nt(al, bh) + _dot_nt(ah, bl)


def _sigmoid(x):
    return 1.0 / (1.0 + jnp.exp(-x))


def _silu(x):
    return x * _sigmoid(x)


def _log_sigmoid(x):
    return jnp.minimum(x, 0.0) - jnp.log(1.0 + jnp.exp(-jnp.abs(x)))


def _layer_norm_rows(z, g, b):
    mu = jnp.mean(z, axis=-1, keepdims=True)
    zc = z - mu
    var = jnp.mean(zc * zc, axis=-1, keepdims=True)
    return zc * lax.rsqrt(var + EPS) * g + b


def _params(sem):
    return pltpu.CompilerParams(dimension_semantics=sem, vmem_limit_bytes=_VMEM_LIMIT_BYTES)


def _mod_kernel(c_ref, w_ref, b_ref, o_ref):
    s = _silu(c_ref[...])
    o_ref[0] = _dot_f32(s, w_ref[0]) + b_ref[0]


def _modulation(cond, ada_w, ada_b):
    tn = 1024
    n = 6 * D_MODEL
    return pl.pallas_call(
        _mod_kernel,
        grid=(DEPTH, n // tn),
        in_specs=[
            pl.BlockSpec((8, D_MODEL), lambda l, j: (0, 0)),
            pl.BlockSpec((1, D_MODEL, tn), lambda l, j: (l, 0, j)),
            pl.BlockSpec((1, 1, tn), lambda l, j: (l, 0, j)),
        ],
        out_specs=pl.BlockSpec((1, 8, tn), lambda l, j: (l, 0, j)),
        out_shape=jax.ShapeDtypeStruct((DEPTH, 8, n), _F32),
        compiler_params=_params(("arbitrary", "arbitrary")),
        name="adaln_mod",
    )(cond, ada_w, ada_b.reshape(DEPTH, 1, n))


def _mod_spec(seq_base, tiles_per_seq):
    if tiles_per_seq is None:
        return pl.BlockSpec((1, 6, D_MODEL), lambda i, *_: (seq_base, 0, 0))
    return pl.BlockSpec((1, 6, D_MODEL), lambda i, *_: (seq_base + i // tiles_per_seq, 0, 0))


def _qkv_kernel(*refs, rope):
    if rope:
        x_ref, m_ref, w_ref, cos_ref, sa_ref, sb_ref, q_ref, k_ref, v_ref = refs
    else:
        x_ref, m_ref, w_ref, q_ref, k_ref, v_ref = refs
    h = (x_ref[...] * (1.0 + m_ref[0, 1:2, :]) + m_ref[0, 0:1, :]).astype(_BF)
    for c, o_ref in enumerate((q_ref, k_ref, v_ref)):
        y = _dot(h, w_ref[:, c * D_MODEL:(c + 1) * D_MODEL])
        if rope and c < 2:
            cos, sa, sb = cos_ref[...], sa_ref[...], sb_ref[...]
            for hd in range(DA_HEADS):
                yh = y[:, hd * 128:(hd + 1) * 128]
                yh = yh * cos + pltpu.roll(yh, 112, 1) * sa + pltpu.roll(yh, 16, 1) * sb
                if c == 0:
                    yh = yh * (DA_DK ** -0.5)
                o_ref[:, hd * 128:(hd + 1) * 128] = yh.astype(o_ref.dtype)
        else:
            if c == 0:
                y = y * (DA_DK ** -0.5)
            o_ref[...] = y.astype(o_ref.dtype)


def _rope_tables(seq_len):
    t = jnp.arange(seq_len)
    row = (t // GRID_W).astype(_F32)
    col = (t % GRID_W).astype(_F32)
    lane = jnp.arange(128)
    d = lane % DA_DK
    half = DA_DK // 2
    nf = half // 2
    dd = d % half
    f = dd % nf
    odd = (dd // nf) == 1
    inv = ROPE_THETA ** (-jnp.arange(nf, dtype=_F32) / nf)
    pos = jnp.where((d < half)[None, :], row[:, None], col[:, None])
    ang = pos * inv[f][None, :]
    cos, sin = jnp.cos(ang), jnp.sin(ang)
    sa = jnp.where(odd[None, :], 0.0, -sin)
    sb = jnp.where(odd[None, :], sin, 0.0)
    return cos, sa, sb


def _qkv_proj(x2d, mods, w_bf, *, seq_len, seq_base, rope, kv_dtype, tm):
    rows = x2d.shape[0]
    tps = seq_len // tm if seq_base else None
    in_specs = [
        pl.BlockSpec((tm, D_MODEL), lambda i: (i, 0)),
        _mod_spec(seq_base, tps),
        pl.BlockSpec((D_MODEL, 3 * D_MODEL), lambda i: (0, 0)),
    ]
    args = [x2d, mods, w_bf]
    if rope:
        tabs = _rope_tables(seq_len)
        in_specs += [pl.BlockSpec((tm, 128), lambda i: (i % (seq_len // tm), 0))] * 3
        args += list(tabs)
    out_spec = pl.BlockSpec((tm, D_MODEL), lambda i: (i, 0))
    return pl.pallas_call(
        functools.partial(_qkv_kernel, rope=rope),
        grid=(rows // tm,),
        in_specs=in_specs,
        out_specs=[out_spec, out_spec, out_spec],
        out_shape=[jax.ShapeDtypeStruct((rows, D_MODEL), _BF),
                   jax.ShapeDtypeStruct((rows, D_MODEL), kv_dtype),
                   jax.ShapeDtypeStruct((rows, D_MODEL), kv_dtype)],
        compiler_params=_params(("arbitrary",)),
        name="da_qkv_rope" if rope else "da_qkv",
    )(*args)


def _attn_kernel(*refs, n_new, tk, lam_init):
    if n_new:
        (q_ref, ka_ref, va_ref, kb_ref, vb_ref, lam_ref, sub_ref, o_ref,
         vta_ref, vtb_ref, acc_ref, m_ref, l_ref) = refs
    else:
        (q_ref, ka_ref, va_ref, lam_ref, sub_ref, o_ref, vta_ref, acc_ref, m_ref, l_ref) = refs
    tq = q_ref.shape[1]

    @pl.when(pl.program_id(2) == 0)
    def _():
        vta_ref[...] = va_ref[0].astype(_F32).T.astype(_BF)
        if n_new:
            for j in range(n_new):
                vtb_ref[j] = vb_ref[0, j * tk:(j + 1) * tk, :].astype(_F32).T.astype(_BF)

    q = q_ref[0]
    m_ref[...] = jnp.full(m_ref.shape, _NEG, _F32)
    l_ref[...] = jnp.zeros(l_ref.shape, _F32)
    acc_ref[...] = jnp.zeros(acc_ref.shape, _F32)

    def tile(k_tile, vt_tile):
        for mp in range(2):
            s = _dot_nt(k_tile[:, mp * DA_DK:(mp + 1) * DA_DK], q[:, mp * DA_DK:(mp + 1) * DA_DK])
            m_old = m_ref[mp:mp + 1, :]
            m_new = jnp.maximum(m_old, jnp.max(s, axis=0, keepdims=True))
            alpha = jnp.exp(m_old - m_new)
            e = jnp.exp(s - m_new)
            l_ref[mp:mp + 1, :] = alpha * l_ref[mp:mp + 1, :] + jnp.sum(e, axis=0, keepdims=True)
            acc_ref[mp] = alpha * acc_ref[mp] + _dot(vt_tile, e.astype(_BF))
            m_ref[mp:mp + 1, :] = m_new

    tile(ka_ref[0].astype(_BF), vta_ref[...])
    if n_new:
        def body(j, carry):
            start = pl.multiple_of(j * tk, tk)
            tile(kb_ref[0, pl.ds(start, tk), :], vtb_ref[j])
            return carry
        lax.fori_loop(0, n_new, body, 0)

    lf = lam_ref[...]
    lam = (jnp.exp(jnp.sum(lf[0:1] * lf[1:2], axis=1, keepdims=True))
           - jnp.exp(jnp.sum(lf[2:3] * lf[3:4], axis=1, keepdims=True)) + lam_init)
    o = acc_ref[0] * (1.0 / l_ref[0:1, :]) - acc_ref[1] * (lam / l_ref[1:2, :])
    ms = jnp.mean(o * o, axis=0, keepdims=True)
    o = o * lax.rsqrt(ms + EPS) * (sub_ref[...] * (1.0 - lam_init))
    o_ref[0] = o.T.astype(o_ref.dtype)


def _diff_attention(q, ka, va, kb, vb, lam, subln, *, lam_init, tq, tk):
    b, lq, _ = q.shape
    la = ka.shape[1]
    n_new = 0 if kb is None else kb.shape[1] // tk
    hspec = lambda rows: pl.BlockSpec((1, rows, 128), lambda bi, hi, qi: (bi, 0, hi))
    in_specs = [pl.BlockSpec((1, tq, 128), lambda bi, hi, qi: (bi, qi, hi)), hspec(la), hspec(la)]
    args = [q, ka, va]
    scratch = [pltpu.VMEM((128, la), _BF)]
    if n_new:
        in_specs += [hspec(kb.shape[1]), hspec(kb.shape[1])]
        args += [kb, vb]
        scratch.append(pltpu.VMEM((n_new, 128, tk), _BF))
    in_specs += [pl.BlockSpec((4, DA_DK), lambda bi, hi, qi: (0, 0)),
                 pl.BlockSpec((DA_DV, 1), lambda bi, hi, qi: (0, 0))]
    args += [lam, subln.reshape(DA_DV, 1)]
    scratch += [pltpu.VMEM((2, 128, tq), _F32), pltpu.VMEM((8, tq), _F32), pltpu.VMEM((8, tq), _F32)]
    return pl.pallas_call(
        functools.partial(_attn_kernel, n_new=n_new, tk=tk, lam_init=lam_init),
        grid=(b, DA_HEADS, lq // tq),
        in_specs=in_specs,
        out_specs=pl.BlockSpec((1, tq, 128), lambda bi, hi, qi: (bi, qi, hi)),
        out_shape=jax.ShapeDtypeStruct((b, lq, D_MODEL), _BF),
        scratch_shapes=scratch,
        compiler_params=_params(("arbitrary", "arbitrary", "arbitrary")),
        name="diff_attn_ctx" if n_new else "diff_attn",
    )(*args)


def _outproj_kernel(*refs, mlstm):
    if mlstm:
        hf_ref, hb_ref, og_ref, nw_ref, w_ref, x_ref, m_ref, g_ref, b_ref, o_ref = refs
        h = hf_ref[...] + hb_ref[...]
        parts = []
        for hd in range(ML_HEADS):
            hh = h[:, hd * ML_DV:(hd + 1) * ML_DV]
            mu = jnp.mean(hh, axis=-1, keepdims=True)
            hc = hh - mu
            var = jnp.mean(hc * hc, axis=-1, keepdims=True)
            parts.append(hc * lax.rsqrt(var + EPS))
        hn = jnp.concatenate(parts, axis=-1) * nw_ref[...]
        a = (og_ref[...].astype(_F32) * hn).astype(_BF)
    else:
        a_ref, w_ref, x_ref, m_ref, g_ref, b_ref, o_ref = refs
        a = a_ref[...]
    y = _dot(a, w_ref[...])
    z = ALPHA * x_ref[...] + m_ref[0, 2:3, :] * y
    o_ref[...] = _layer_norm_rows(z, g_ref[...], b_ref[...])


def _outproj(acts, w_bf, x2d, mods, ln_g, ln_b, *, seq_len, seq_base, tm, norm_w=None):
    rows = x2d.shape[0]
    mlstm = norm_w is not None
    tps = seq_len // tm if seq_base else None
    row_spec = pl.BlockSpec((tm, D_MODEL), lambda i: (i, 0))
    vec_spec = pl.BlockSpec((1, D_MODEL), lambda i: (0, 0))
    in_specs = [row_spec] * len(acts)
    args = list(acts)
    if mlstm:
        in_specs.append(vec_spec)
        args.append(norm_w.reshape(1, D_MODEL))
    in_specs += [pl.BlockSpec((D_MODEL, D_MODEL), lambda i: (0, 0)), row_spec, _mod_spec(seq_base, tps),
                 vec_spec, vec_spec]
    args += [w_bf, x2d, mods, ln_g.reshape(1, D_MODEL), ln_b.reshape(1, D_MODEL)]
    return pl.pallas_call(
        functools.partial(_outproj_kernel, mlstm=mlstm),
        grid=(rows // tm,),
        in_specs=in_specs,
        out_specs=row_spec,
        out_shape=jax.ShapeDtypeStruct((rows, D_MODEL), _F32),
        compiler_params=_params(("arbitrary",)),
        name="ml_outproj_ln" if mlstm else "da_outproj_ln",
    )(*args)


def _halo_specs(rows, tm):
    per = tm // _HALO
    nblk = rows // _HALO
    main = pl.BlockSpec((tm, D_MODEL), lambda i, *_: (i, 0))
    prev = pl.BlockSpec((_HALO, D_MODEL), lambda i, *_: (jnp.maximum(i * per - 1, 0), 0))
    nxt = pl.BlockSpec((_HALO, D_MODEL), lambda i, *_: (jnp.minimum((i + 1) * per, nblk - 1), 0))
    return prev, main, nxt


def _modulated_ext(prev_ref, main_ref, next_ref, shift, scale):
    ext = jnp.concatenate([prev_ref[...], main_ref[...], next_ref[...]], axis=0)
    return (ext * (1.0 + scale) + shift).astype(_BF)


def _conv3(u_ref, w_ref, b_ref, tm, not_first, not_last):
    up = jnp.where(not_first, u_ref[pl.ds(_HALO - 1, tm), :], 0.0)
    uc = u_ref[pl.ds(_HALO, tm), :]
    un = jnp.where(not_last, u_ref[pl.ds(_HALO + 1, tm), :], 0.0)
    return up * w_ref[0:1, :] + uc * w_ref[1:2, :] + un * w_ref[2:3, :] + b_ref[...]


def _seq_edge_masks(tm, seq_len):
    pos = (pl.program_id(0) * tm + lax.broadcasted_iota(jnp.int32, (tm, 1), 0)) % seq_len
    return pos != 0, pos != seq_len - 1


def _ffn_kernel(prev_ref, x_ref, next_ref, m_ref, wg_ref, wv_ref, cwg_ref, cwv_ref, cbg_ref, cbv_ref,
                wd_ref, g_ref, b_ref, o_ref, h_ref, ug_ref, uv_ref, acc_ref, *, seq_len):
    j = pl.program_id(1)
    tm = x_ref.shape[0]

    @pl.when(j == 0)
    def _():
        h_ref[...] = _modulated_ext(prev_ref, x_ref, next_ref, m_ref[0, 3:4, :], m_ref[0, 4:5, :])
        acc_ref[...] = jnp.zeros(acc_ref.shape, _F32)

    not_first, not_last = _seq_edge_masks(tm, seq_len)
    h = h_ref[...]
    ug_ref[...] = _dot(h, wg_ref[...])
    uv_ref[...] = _dot(h, wv_ref[...])
    gate = _conv3(ug_ref, cwg_ref, cbg_ref, tm, not_first, not_last)
    val = _conv3(uv_ref, cwv_ref, cbv_ref, tm, not_first, not_last)
    acc_ref[...] += _dot((_silu(gate) * val).astype(_BF), wd_ref[...])

    @pl.when(j == pl.num_programs(1) - 1)
    def _():
        z = ALPHA * x_ref[...] + m_ref[0, 5:6, :] * acc_ref[...]
        o_ref[...] = _layer_norm_rows(z, g_ref[...], b_ref[...])


def _conv_ffn(x2d, mods, w_up_bf, conv_w, conv_b, w_down_bf, ln_g, ln_b, *, seq_len, seq_base, tm):
    rows = x2d.shape[0]
    nj = D_FF // _FF_CHUNK
    tps = seq_len // tm if seq_base else None
    prev, main, nxt = _halo_specs(rows, tm)
    vec_spec = pl.BlockSpec((1, D_MODEL), lambda i, j: (0, 0))
    conv_b2 = conv_b.reshape(1, 2 * D_FF)
    return pl.pallas_call(
        functools.partial(_ffn_kernel, seq_len=seq_len),
        grid=(rows // tm, nj),
        in_specs=[
            prev, main, nxt, _mod_spec(seq_base, tps),
            pl.BlockSpec((D_MODEL, _FF_CHUNK), lambda i, j: (0, j)),
            pl.BlockSpec((D_MODEL, _FF_CHUNK), lambda i, j: (0, nj + j)),
            pl.BlockSpec((3, _FF_CHUNK), lambda i, j: (0, j)),
            pl.BlockSpec((3, _FF_CHUNK), lambda i, j: (0, nj + j)),
            pl.BlockSpec((1, _FF_CHUNK), lambda i, j: (0, j)),
            pl.BlockSpec((1, _FF_CHUNK), lambda i, j: (0, nj + j)),
            pl.BlockSpec((_FF_CHUNK, D_MODEL), lambda i, j: (j, 0)),
            vec_spec, vec_spec,
        ],
        out_specs=pl.BlockSpec((tm, D_MODEL), lambda i, j: (i, 0)),
        out_shape=jax.ShapeDtypeStruct((rows, D_MODEL), _F32),
        scratch_shapes=[
            pltpu.VMEM((tm + 2 * _HALO, D_MODEL), _BF),
            pltpu.VMEM((tm + 2 * _HALO, _FF_CHUNK), _F32),
            pltpu.VMEM((tm + 2 * _HALO, _FF_CHUNK), _F32),
            pltpu.VMEM((tm, D_MODEL), _F32),
        ],
        compiler_params=_params(("arbitrary", "arbitrary")),
        name="conv_ffn_ln",
    )(x2d, x2d, x2d, mods, w_up_bf, w_up_bf, conv_w, conv_w, conv_b2, conv_b2, w_down_bf,
      ln_g.reshape(1, D_MODEL), ln_b.reshape(1, D_MODEL))


def _mlproj_kernel(prev_ref, x_ref, next_ref, m_ref, w_ref, wg_ref, wgt_ref, cw_ref, cb_ref, bg_ref, bgt_ref,
                   q_ref, k_ref, v_ref, og_ref, gt_ref, gtt_ref, u_ref, *, seq_len):
    tm = x_ref.shape[0]
    shift, scale = m_ref[0, 0:1, :], m_ref[0, 1:2, :]
    h = _modulated_ext(prev_ref, x_ref, next_ref, shift, scale)
    u_ref[...] = _dot(h, w_ref[:, 0:2 * ML_QK])
    not_first, not_last = _seq_edge_masks(tm, seq_len)
    qk = _silu(_conv3(u_ref, cw_ref, cb_ref, tm, not_first, not_last))
    q_ref[...] = qk[:, 0:ML_QK].astype(_BF)
    k_ref[...] = (qk[:, ML_QK:2 * ML_QK] * (ML_DK ** -0.5)).astype(_BF)
    hm = h[_HALO:_HALO + tm, :]
    v_ref[...] = _dot(hm, w_ref[:, 2 * ML_QK:2 * ML_QK + ML_V]).astype(_BF)
    og_ref[...] = _sigmoid(_dot(hm, w_ref[:, 2 * ML_QK + ML_V:2 * ML_QK + 2 * ML_V])).astype(_BF)
    hf = x_ref[...] * (1.0 + scale) + shift
    gt_ref[...] = _dot_f32(hf, wg_ref[...]) + bg_ref[...]
    gtt_ref[...] = _dot_nt_f32(wgt_ref[...], hf) + bgt_ref[...]


def _ml_proj(x2d, mods, w_main_bf, w_gate, conv_w, conv_b, b_gate, *, seq_len, seq_base, tm):
    rows = x2d.shape[0]
    tps = seq_len // tm if seq_base else None
    prev, main, nxt = _halo_specs(rows, tm)
    n_main = 2 * ML_QK + 2 * ML_V
    full = lambda shape: pl.BlockSpec(shape, lambda i: (0,) * len(shape))
    row = lambda n: pl.BlockSpec((tm, n), lambda i: (i, 0))
    return pl.pallas_call(
        functools.partial(_mlproj_kernel, seq_len=seq_len),
        grid=(rows // tm,),
        in_specs=[prev, main, nxt, _mod_spec(seq_base, tps), full((D_MODEL, n_main)),
                  full((D_MODEL, ML_GATES)), full((ML_GATES, D_MODEL)), full((3, 2 * ML_QK)),
                  full((1, 2 * ML_QK)), full((1, ML_GATES)), full((ML_GATES, 1))],
        out_specs=[row(ML_QK), row(ML_QK), row(ML_V), row(ML_V), row(ML_GATES),
                   pl.BlockSpec((ML_GATES, tm), lambda i: (0, i))],
        out_shape=[jax.ShapeDtypeStruct((rows, ML_QK), _BF), jax.ShapeDtypeStruct((rows, ML_QK), _BF),
                   jax.ShapeDtypeStruct((rows, ML_V), _BF), jax.ShapeDtypeStruct((rows, ML_V), _BF),
                   jax.ShapeDtypeStruct((rows, ML_GATES), _F32),
                   jax.ShapeDtypeStruct((ML_GATES, rows), _F32)],
        scratch_shapes=[pltpu.VMEM((tm + 2 * _HALO, 2 * ML_QK), _F32)],
        compiler_params=_params(("arbitrary",)),
        name="ml_inproj_conv",
    )(x2d, x2d, x2d, mods, w_main_bf, w_gate, w_gate.T, conv_w, conv_b.reshape(1, 2 * ML_QK),
      b_gate.reshape(1, ML_GATES), b_gate.reshape(ML_GATES, 1))


def _mlstm_chunk(q, k, v, i_col, i_row, b_col, b_row, total, m_prev, c_prev, n_prev, causal):
    t = q.shape[0]
    ti = lax.broadcasted_iota(jnp.int32, (t, t), 0)
    si = lax.broadcasted_iota(jnp.int32, (t, t), 1)
    mask = (si <= ti) if causal else (si >= ti)
    dmat = jnp.where(mask, b_col - b_row + i_row, _NEG)
    inter = b_col + m_prev
    m_row = jnp.maximum(inter, jnp.max(dmat, axis=1, keepdims=True))
    w_intra = jnp.exp(dmat - m_row)
    w_inter = jnp.exp(inter - m_row)
    s = _dot_nt(q, k) * w_intra
    num = w_inter * _dot(q, c_prev.astype(_BF)) + _dot(s.astype(_BF), v)
    qn = jnp.sum(q.astype(_F32) * n_prev, axis=1, keepdims=True)
    den = w_inter * qn + jnp.sum(s, axis=1, keepdims=True)
    h = num / jnp.maximum(jnp.abs(den), jnp.exp(-m_row))
    dec = total - b_col + i_col
    m_new = jnp.maximum(total + m_prev, jnp.max(dec, axis=0, keepdims=True))
    w_s = jnp.exp(dec - m_new)
    carry = jnp.exp(total + m_prev - m_new)
    kw = k.astype(_F32) * w_s
    c_new = carry * c_prev + _dot(kw.T.astype(_BF), v)
    n_new = carry * n_prev + jnp.sum(kw, axis=0, keepdims=True)
    return h, c_new, n_new, m_new


def _mlstm_kernel(*refs, has_init, want_state):
    refs = list(refs)
    (qf_ref, kf_ref, vf_ref, gf_ref, gtf_ref, qb_ref, kb_ref, vb_ref, gb_ref, gtb_ref) = refs[:10]
    pos = 10
    if has_init:
        c0_ref, n0_ref, m0_ref = refs[pos:pos + 3]
        pos += 3
    hf_ref, hb_ref = refs[pos:pos + 2]
    pos += 2
    if want_state:
        co_ref, no_ref, mo_ref = refs[pos:pos + 3]
        pos += 3
    c_ref, n_ref, m_ref = refs[pos:pos + 3]
    ci = pl.program_id(1)
    t = qf_ref.shape[1]

    @pl.when(ci == 0)
    def _():
        if has_init:
            c_ref[...] = c0_ref[0]
            n_ref[...] = n0_ref[0]
            m_ref[...] = jnp.broadcast_to(m0_ref[0], m_ref.shape)
        else:
            c_ref[...] = jnp.zeros(c_ref.shape, _F32)
            n_ref[...] = jnp.zeros(n_ref.shape, _F32)
            m_ref[...] = jnp.zeros(m_ref.shape, _F32)

    ri = lax.broadcasted_iota(jnp.int32, (t, t), 0)
    cj = lax.broadcasted_iota(jnp.int32, (t, t), 1)
    lower = (cj <= ri).astype(_BF)
    upper = (cj >= ri).astype(_BF)

    def cum(mat, x):
        return sum(_dot(mat, p) for p in _split3(x))

    def cum_t(x, mat):
        return sum(_dot(p, mat) for p in _split3(x))

    for d, (q_ref, k_ref, v_ref, g_ref, gt_ref, h_ref) in enumerate(
            ((qf_ref, kf_ref, vf_ref, gf_ref, gtf_ref, hf_ref),
             (qb_ref, kb_ref, vb_ref, gb_ref, gtb_ref, hb_ref))):
        g = g_ref[0]
        gt = gt_ref[...]
        is_f_col = (lax.broadcasted_iota(jnp.int32, (1, ML_GATES), 1) % 8) >= 4
        is_f_row = (lax.broadcasted_iota(jnp.int32, (ML_GATES, 1), 0) % 8) >= 4
        x = jnp.where(is_f_col, _log_sigmoid(g), g)
        xt = jnp.where(is_f_row, _log_sigmoid(gt), gt)
        if d == 0:
            bc = cum(lower, x)
            br = cum_t(xt, upper)
        else:
            bc = cum(upper, x)
            br = cum_t(xt, lower)
        tot = jnp.sum(x, axis=0, keepdims=True)
        for hd in range(ML_HEADS):
            ic, fc = d * 8 + hd, d * 8 + 4 + hd
            st = d * ML_HEADS + hd
            h, c_new, n_new, m_new = _mlstm_chunk(
                q_ref[0, :, hd * ML_DK:(hd + 1) * ML_DK], k_ref[0, :, hd * ML_DK:(hd + 1) * ML_DK],
                v_ref[0, :, hd * ML_DV:(hd + 1) * ML_DV],
                x[:, ic:ic + 1], xt[ic:ic + 1, :], bc[:, fc:fc + 1], br[fc:fc + 1, :],
                tot[:, fc:fc + 1], m_ref[st:st + 1, 0:1], c_ref[st], n_ref[st:st + 1, :], d == 0)
            h_ref[0, :, hd * ML_DV:(hd + 1) * ML_DV] = h
            c_ref[st] = c_new
            n_ref[st:st + 1, :] = n_new
            m_ref[st:st + 1, :] = jnp.broadcast_to(m_new, (1, 128))

    if want_state:
        @pl.when(ci == pl.num_programs(1) - 1)
        def _():
            co_ref[0] = c_ref[...]
            no_ref[0] = n_ref[...]
            mo_ref[0] = m_ref[...]


def _mlstm_scan(q, k, v, gates, gates_t, init, *, want_state):
    b, l, _ = q.shape
    t = _ML_CHUNK
    nc = l // t
    fwd = lambda n: pl.BlockSpec((1, t, n), lambda bi, ci: (bi, ci, 0))
    bwd = lambda n: pl.BlockSpec((1, t, n), lambda bi, ci: (bi, nc - 1 - ci, 0))
    gtf = pl.BlockSpec((ML_GATES, t), lambda bi, ci: (0, bi * nc + ci))
    gtb = pl.BlockSpec((ML_GATES, t), lambda bi, ci: (0, bi * nc + nc - 1 - ci))
    in_specs = [fwd(ML_QK), fwd(ML_QK), fwd(ML_V), fwd(ML_GATES), gtf,
                bwd(ML_QK), bwd(ML_QK), bwd(ML_V), bwd(ML_GATES), gtb]
    args = [q, k, v, gates, gates_t, q, k, v, gates, gates_t]
    has_init = init is not None
    if has_init:
        c0, n0, m0 = init
        in_specs += [pl.BlockSpec((1, 8, ML_DK, ML_DV), lambda bi, ci: (bi, 0, 0, 0)),
                     pl.BlockSpec((1, 8, ML_DK), lambda bi, ci: (bi, 0, 0)),
                     pl.BlockSpec((1, 8, 1), lambda bi, ci: (bi, 0, 0))]
        args += [c0.reshape(b, 8, ML_DK, ML_DV), n0.reshape(b, 8, ML_DK), m0.reshape(b, 8, 1)]
    out_specs = [fwd(ML_V), bwd(ML_V)]
    out_shape = [jax.ShapeDtypeStruct((b, l, ML_V), _F32), jax.ShapeDtypeStruct((b, l, ML_V), _F32)]
    if want_state:
        out_specs += [pl.BlockSpec((1, 8, ML_DK, ML_DV), lambda bi, ci: (bi, 0, 0, 0)),
                      pl.BlockSpec((1, 8, ML_DK), lambda bi, ci: (bi, 0, 0)),
                      pl.BlockSpec((1, 8, 128), lambda bi, ci: (bi, 0, 0))]
        out_shape += [jax.ShapeDtypeStruct((b, 8, ML_DK, ML_DV), _F32),
                      jax.ShapeDtypeStruct((b, 8, ML_DK), _F32),
                      jax.ShapeDtypeStruct((b, 8, 128), _F32)]
    return pl.pallas_call(
        functools.partial(_mlstm_kernel, has_init=has_init, want_state=want_state),
        grid=(b, nc),
        in_specs=in_specs,
        out_specs=out_specs,
        out_shape=out_shape,
        scratch_shapes=[pltpu.VMEM((8, ML_DK, ML_DV), _F32), pltpu.VMEM((8, ML_DK), _F32),
                        pltpu.VMEM((8, 128), _F32)],
        compiler_params=_params(("arbitrary", "arbitrary")),
        name="mlstm_scan_state" if want_state else "mlstm_scan",
    )(*args)


def kernel(x_prompt, x_sample, c, cache_k, cache_v, state_C, state_n, state_m, c_ctx, ada_w, ada_b, ln_g, ln_b,
           da_w_qkv, da_lam, da_subln, da_w_o, ml_w_in, ml_conv_w, ml_conv_b, ml_b_gate, ml_norm_w, ml_w_out,
           ffn_w_up, ffn_conv_w, ffn_conv_b, ffn_w_down):
    bp, lp, _ = x_prompt.shape
    bs, ls, _ = x_sample.shape
    cond = jnp.concatenate([c_ctx[None, :], c, jnp.zeros((8 - 1 - bs, D_MODEL), _F32)], axis=0)
    mods = _modulation(cond, ada_w, ada_b).reshape(DEPTH, 8, 6, D_MODEL)

    xp = x_prompt.reshape(bp * lp, D_MODEL)
    xs = x_sample.reshape(bs * ls, D_MODEL)
    groups = (dict(seq_len=lp, seq_base=0), dict(seq_len=ls, seq_base=1))

    lam_init = 0.8 - 0.6 * math.exp(-0.3 * 0)
    w_qkv = da_w_qkv[0].astype(_BF)
    w_o = da_w_o[0].astype(_BF)
    qp, kp, vp = _qkv_proj(xp, mods[0], w_qkv, rope=False, kv_dtype=_F32, tm=512, **groups[0])
    qs, ks, vs = _qkv_proj(xs, mods[0], w_qkv, rope=True, kv_dtype=_BF, tm=512, **groups[1])
    as3 = lambda a, b: a.reshape(b, -1, D_MODEL)
    op = _diff_attention(as3(qp, bp), as3(kp, bp), as3(vp, bp), None, None, da_lam[0], da_subln[0],
                         lam_init=lam_init, tq=256, tk=512)
    os_ = _diff_attention(as3(qs, bs), cache_k[:, 0].reshape(bs, -1, D_MODEL),
                          cache_v[:, 0].reshape(bs, -1, D_MODEL), as3(ks, bs), as3(vs, bs),
                          da_lam[0], da_subln[0], lam_init=lam_init, tq=256, tk=512)
    xp = _outproj([op.reshape(-1, D_MODEL)], w_o, xp, mods[0], ln_g[0, 0], ln_b[0, 0], tm=512, **groups[0])
    xs = _outproj([os_.reshape(-1, D_MODEL)], w_o, xs, mods[0], ln_g[0, 0], ln_b[0, 0], tm=512, **groups[1])
    new_k = kp.reshape(bp, 1, lp, DA_HEADS, 2, DA_DK)
    new_v = vp.reshape(bp, 1, lp, DA_HEADS, DA_DV)

    def ffn(x2d, i, grp):
        return _conv_ffn(x2d, mods[i], ffn_w_up[i].astype(_BF), ffn_conv_w[i], ffn_conv_b[i],
                         ffn_w_down[i].astype(_BF), ln_g[i, 1], ln_b[i, 1], tm=512, **grp)

    xp = ffn(xp, 0, groups[0])
    xs = ffn(xs, 0, groups[1])

    n_main = 2 * ML_QK + 2 * ML_V
    w_main = ml_w_in[0][:, :n_main].astype(_BF)
    w_gate = ml_w_in[0][:, n_main:]
    w_out = ml_w_out[0].astype(_BF)
    outs = []
    for x2d, grp, nb, init in ((xp, groups[0], bp, None),
                               (xs, groups[1], bs, (state_C[:, 0], state_n[:, 0], state_m[:, 0]))):
        q, k, v, og, gt, gtt = _ml_proj(x2d, mods[1], w_main, w_gate, ml_conv_w[0], ml_conv_b[0], ml_b_gate[0],
                                        tm=512, **grp)
        r3 = lambda a: a.reshape(nb, -1, a.shape[-1])
        res = _mlstm_scan(r3(q), r3(k), r3(v), r3(gt), gtt, init, want_state=init is None)
        hf, hb = res[0].reshape(-1, ML_V), res[1].reshape(-1, ML_V)
        x2d = _outproj([hf, hb, og], w_out, x2d, mods[1], ln_g[1, 0], ln_b[1, 0], tm=512,
                       norm_w=ml_norm_w[0], **grp)
        outs.append((ffn(x2d, 1, grp), res[2:]))
    (xp, (c_fin, n_fin, m_fin)), (xs, _) = outs
    new_c = c_fin.reshape(bp, 1, 2, ML_HEADS, ML_DK, ML_DV)
    new_n = n_fin.reshape(bp, 1, 2, ML_HEADS, ML_DK)
    new_m = m_fin[:, :, 0].reshape(bp, 1, 2, ML_HEADS)
    return (xp.reshape(bp, lp, D_MODEL), xs.reshape(bs, ls, D_MODEL), new_k, new_v, new_c, new_n, new_m)
```

```python
import functools
import math

import jax
import jax.numpy as jnp
from jax import lax
from jax.experimental import pallas as pl
from jax.experimental.pallas import tpu as pltpu

D_MODEL = 1024
DEPTH = 2
GRID_W = 64
DA_HEADS = 8
DA_DK = 64
DA_DV = 128
ML_HEADS = 4
ML_DK = 128
ML_DV = 256
ML_QK = ML_HEADS * ML_DK
ML_V = ML_HEADS * ML_DV
ML_GATES = 16
D_FF = 2816
ROPE_THETA = 10000.0
ALPHA = (2 * DEPTH) ** 0.25
EPS = 1e-5

_BF = jnp.bfloat16
_F32 = jnp.float32
_NEG = -1e30

_VMEM_LIMIT_BYTES = 56 * 1024 * 1024
_HALO = 16
_ML_CHUNK = 256
_ROW_TILE = 512
_FF_CHUNK = 256
_Q_SCALE = DA_DK ** -0.5 * math.log2(math.e)


def _dot(a, b):
    return jnp.dot(a, b, preferred_element_type=_F32)


def _dot_nt(a, b):
    return lax.dot_general(a, b, (((1,), (1,)), ((), ())), preferred_element_type=_F32)


def _split2(x):
    hi = x.astype(_BF)
    lo = (x - hi.astype(_F32)).astype(_BF)
    return hi, lo


def _split3(x):
    hi = x.astype(_BF)
    r = x - hi.astype(_F32)
    mid = r.astype(_BF)
    lo = (r - mid.astype(_F32)).astype(_BF)
    return hi, mid, lo


def _dot_f32(a, b):
    ah, al = _split2(a)
    bh, bl = _split2(b)
    return _dot(ah, bh) + _dot(al, bh) + _dot(ah, bl)


def _dot_nt_f32(a, b):
    ah, al = _split2(a)
    bh, bl = _split2(b)
    return _dot_nt(ah, bh) + _dot_nt(al, bh) + _dot_nt(ah, bl)


def _sigmoid(x):
    return 1.0 / (1.0 + jnp.exp(-x))


def _silu(x):
    return x * _sigmoid(x)


def _log_sigmoid(x):
    return jnp.minimum(x, 0.0) - jnp.log(1.0 + jnp.exp(-jnp.abs(x)))


def _layer_norm_rows(z, g, b):
    mu = jnp.mean(z, axis=-1, keepdims=True)
    zc = z - mu
    var = jnp.mean(zc * zc, axis=-1, keepdims=True)
    return zc * lax.rsqrt(var + EPS) * g + b


def _params(sem):
    return pltpu.CompilerParams(dimension_semantics=sem, vmem_limit_bytes=_VMEM_LIMIT_BYTES)


def _mod_kernel(c_ref, w_ref, b_ref, o_ref):
    s = _silu(c_ref[...])
    o_ref[0] = _dot_f32(s, w_ref[0]) + b_ref[0]


def _modulation(cond, ada_w, ada_b):
    tn = 1024
    n = 6 * D_MODEL
    return pl.pallas_call(
        _mod_kernel,
        grid=(DEPTH, n // tn),
        in_specs=[
            pl.BlockSpec((8, D_MODEL), lambda l, j: (0, 0)),
            pl.BlockSpec((1, D_MODEL, tn), lambda l, j: (l, 0, j)),
            pl.BlockSpec((1, 1, tn), lambda l, j: (l, 0, j)),
        ],
        out_specs=pl.BlockSpec((1, 8, tn), lambda l, j: (l, 0, j)),
        out_shape=jax.ShapeDtypeStruct((DEPTH, 8, n), _F32),
        compiler_params=_params(("arbitrary", "arbitrary")),
        name="adaln_mod",
    )(cond, ada_w, ada_b.reshape(DEPTH, 1, n))


def _mod_spec(seq_base, tiles_per_seq):
    if tiles_per_seq is None:
        return pl.BlockSpec((1, 6, D_MODEL), lambda i, *_: (seq_base, 0, 0))
    return pl.BlockSpec((1, 6, D_MODEL), lambda i, *_: (seq_base + i // tiles_per_seq, 0, 0))


def _qkv_kernel(*refs, rope):
    if rope:
        x_ref, m_ref, w_ref, cos_ref, sa_ref, sb_ref, q_ref, k_ref, v_ref = refs
    else:
        x_ref, m_ref, w_ref, q_ref, k_ref, v_ref = refs
    h = (x_ref[...] * (1.0 + m_ref[0, 1:2, :]) + m_ref[0, 0:1, :]).astype(_BF)
    for c, o_ref in enumerate((q_ref, k_ref, v_ref)):
        y = _dot(h, w_ref[:, c * D_MODEL:(c + 1) * D_MODEL])
        if rope and c < 2:
            cos, sa, sb = cos_ref[...], sa_ref[...], sb_ref[...]
            for hd in range(DA_HEADS):
                yh = y[:, hd * 128:(hd + 1) * 128]
                yh = yh * cos + pltpu.roll(yh, 112, 1) * sa + pltpu.roll(yh, 16, 1) * sb
                if c == 0:
                    yh = yh * _Q_SCALE
                o_ref[:, hd * 128:(hd + 1) * 128] = yh.astype(o_ref.dtype)
        else:
            if c == 0:
                y = y * _Q_SCALE
            o_ref[...] = y.astype(o_ref.dtype)


def _rope_tables(seq_len):
    t = jnp.arange(seq_len)
    row = (t // GRID_W).astype(_F32)
    col = (t % GRID_W).astype(_F32)
    lane = jnp.arange(128)
    d = lane % DA_DK
    half = DA_DK // 2
    nf = half // 2
    dd = d % half
    f = dd % nf
    odd = (dd // nf) == 1
    inv = ROPE_THETA ** (-jnp.arange(nf, dtype=_F32) / nf)
    pos = jnp.where((d < half)[None, :], row[:, None], col[:, None])
    ang = pos * inv[f][None, :]
    cos, sin = jnp.cos(ang), jnp.sin(ang)
    sa = jnp.where(odd[None, :], 0.0, -sin)
    sb = jnp.where(odd[None, :], sin, 0.0)
    return cos, sa, sb


def _qkv_proj(x2d, mods, w_bf, *, seq_len, seq_base, rope, kv_dtype, tm):
    rows = x2d.shape[0]
    tps = seq_len // tm if seq_base else None
    in_specs = [
        pl.BlockSpec((tm, D_MODEL), lambda i: (i, 0)),
        _mod_spec(seq_base, tps),
        pl.BlockSpec((D_MODEL, 3 * D_MODEL), lambda i: (0, 0)),
    ]
    args = [x2d, mods, w_bf]
    if rope:
        tabs = _rope_tables(seq_len)
        in_specs += [pl.BlockSpec((tm, 128), lambda i: (i % (seq_len // tm), 0))] * 3
        args += list(tabs)
    out_spec = pl.BlockSpec((tm, D_MODEL), lambda i: (i, 0))
    return pl.pallas_call(
        functools.partial(_qkv_kernel, rope=rope),
        grid=(rows // tm,),
        in_specs=in_specs,
        out_specs=[out_spec, out_spec, out_spec],
        out_shape=[jax.ShapeDtypeStruct((rows, D_MODEL), _BF),
                   jax.ShapeDtypeStruct((rows, D_MODEL), kv_dtype),
                   jax.ShapeDtypeStruct((rows, D_MODEL), kv_dtype)],
        compiler_params=_params(("arbitrary",)),
        name="da_qkv_rope" if rope else "da_qkv",
    )(*args)


def _group_rows(x):
    return x.reshape(x.shape[0] // 8, 8, x.shape[1])


def _attn_kernel(*refs, n_new, tk, lam_init):
    if n_new:
        (q_ref, ka_ref, va_ref, kb_ref, vb_ref, lam_ref, sub_ref, o_ref,
         vta_ref, vtb_ref, s_ref, acc_ref) = refs
    else:
        (q_ref, ka_ref, va_ref, lam_ref, sub_ref, o_ref, vta_ref, s_ref, acc_ref) = refs
    la = ka_ref.shape[1]

    @pl.when(pl.program_id(2) == 0)
    def _():
        vta_ref[...] = va_ref[0].astype(_F32).T.astype(_BF)
        if n_new:
            for j in range(n_new):
                vtb_ref[j] = vb_ref[0, j * tk:(j + 1) * tk, :].astype(_F32).T.astype(_BF)

    q = q_ref[0]
    first_map = lax.broadcasted_iota(jnp.int32, q.shape, 1) < DA_DK
    qm = (jnp.where(first_map, q, jnp.zeros_like(q)), jnp.where(first_map, jnp.zeros_like(q), q))

    ka = ka_ref[0].astype(_BF)
    mrun = []
    for mp in range(2):
        s = _dot_nt(ka, qm[mp])
        s_ref[mp, 0:la, :] = s
        mrun.append(jnp.max(_group_rows(s), axis=0))
    if n_new:
        def score_tile(j, carry):
            start = pl.multiple_of(j * tk, tk)
            kt = kb_ref[0, pl.ds(start, tk), :]
            out = []
            for mp in range(2):
                s = _dot_nt(kt, qm[mp])
                s_ref[mp, pl.ds(la + start, tk), :] = s
                out.append(jnp.maximum(carry[mp], jnp.max(_group_rows(s), axis=0)))
            return tuple(out)
        mrun = lax.fori_loop(0, n_new, score_tile, tuple(mrun), unroll=True)
    mx = [jnp.max(x, axis=0, keepdims=True) for x in mrun]

    lsum = []
    for mp in range(2):
        e = jnp.exp2(s_ref[mp, 0:la, :] - mx[mp])
        acc_ref[mp] = _dot(vta_ref[...], e.astype(_BF))
        lsum.append(jnp.sum(_group_rows(e), axis=0))
    if n_new:
        def value_tile(j, carry):
            start = pl.multiple_of(j * tk, tk)
            out = []
            for mp in range(2):
                e = jnp.exp2(s_ref[mp, pl.ds(la + start, tk), :] - mx[mp])
                acc_ref[mp] += _dot(vtb_ref[j], e.astype(_BF))
                out.append(carry[mp] + jnp.sum(_group_rows(e), axis=0))
            return tuple(out)
        lsum = lax.fori_loop(0, n_new, value_tile, tuple(lsum), unroll=True)
    l0, l1 = [jnp.sum(x, axis=0, keepdims=True) for x in lsum]

    lf = lam_ref[...]
    lam = (jnp.exp(jnp.sum(lf[0:1] * lf[1:2], axis=1, keepdims=True))
           - jnp.exp(jnp.sum(lf[2:3] * lf[3:4], axis=1, keepdims=True)) + lam_init)
    o = acc_ref[0] * (1.0 / l0) - acc_ref[1] * (lam / l1)
    ms = jnp.mean(o * o, axis=0, keepdims=True)
    o = o * lax.rsqrt(ms + EPS) * (sub_ref[...] * (1.0 - lam_init))
    o_ref[0] = o.T.astype(o_ref.dtype)


def _diff_attention(q, ka, va, kb, vb, lam, subln, *, lam_init, tq, tk):
    b, lq, _ = q.shape
    la = ka.shape[1]
    n_new = 0 if kb is None else kb.shape[1] // tk
    hspec = lambda rows: pl.BlockSpec((1, rows, 128), lambda bi, hi, qi: (bi, 0, hi))
    in_specs = [pl.BlockSpec((1, tq, 128), lambda bi, hi, qi: (bi, qi, hi)), hspec(la), hspec(la)]
    args = [q, ka, va]
    scratch = [pltpu.VMEM((128, la), _BF)]
    if n_new:
        in_specs += [hspec(kb.shape[1]), hspec(kb.shape[1])]
        args += [kb, vb]
        scratch.append(pltpu.VMEM((n_new, 128, tk), _BF))
    in_specs += [pl.BlockSpec((4, DA_DK), lambda bi, hi, qi: (0, 0)),
                 pl.BlockSpec((DA_DV, 1), lambda bi, hi, qi: (0, 0))]
    args += [lam, subln.reshape(DA_DV, 1)]
    lk = la + (kb.shape[1] if n_new else 0)
    scratch += [pltpu.VMEM((2, lk, tq), _F32), pltpu.VMEM((2, 128, tq), _F32)]
    return pl.pallas_call(
        functools.partial(_attn_kernel, n_new=n_new, tk=tk, lam_init=lam_init),
        grid=(b, DA_HEADS, lq // tq),
        in_specs=in_specs,
        out_specs=pl.BlockSpec((1, tq, 128), lambda bi, hi, qi: (bi, qi, hi)),
        out_shape=jax.ShapeDtypeStruct((b, lq, D_MODEL), _BF),
        scratch_shapes=scratch,
        compiler_params=_params(("arbitrary", "arbitrary", "arbitrary")),
        name="diff_attn_ctx" if n_new else "diff_attn",
    )(*args)


def _outproj_kernel(*refs, mlstm):
    if mlstm:
        hf_ref, hb_ref, og_ref, nw_ref, w_ref, x_ref, m_ref, g_ref, b_ref, o_ref = refs
        h = hf_ref[...] + hb_ref[...]
        parts = []
        for hd in range(ML_HEADS):
            hh = h[:, hd * ML_DV:(hd + 1) * ML_DV]
            mu = jnp.mean(hh, axis=-1, keepdims=True)
            hc = hh - mu
            var = jnp.mean(hc * hc, axis=-1, keepdims=True)
            parts.append(hc * lax.rsqrt(var + EPS))
        hn = jnp.concatenate(parts, axis=-1) * nw_ref[...]
        a = (og_ref[...].astype(_F32) * hn).astype(_BF)
    else:
        a_ref, w_ref, x_ref, m_ref, g_ref, b_ref, o_ref = refs
        a = a_ref[...]
    y = _dot(a, w_ref[...])
    z = ALPHA * x_ref[...] + m_ref[0, 2:3, :] * y
    o_ref[...] = _layer_norm_rows(z, g_ref[...], b_ref[...])


def _outproj(acts, w_bf, x2d, mods, ln_g, ln_b, *, seq_len, seq_base, tm, norm_w=None):
    rows = x2d.shape[0]
    mlstm = norm_w is not None
    tps = seq_len // tm if seq_base else None
    row_spec = pl.BlockSpec((tm, D_MODEL), lambda i: (i, 0))
    vec_spec = pl.BlockSpec((1, D_MODEL), lambda i: (0, 0))
    in_specs = [row_spec] * len(acts)
    args = list(acts)
    if mlstm:
        in_specs.append(vec_spec)
        args.append(norm_w.reshape(1, D_MODEL))
    in_specs += [pl.BlockSpec((D_MODEL, D_MODEL), lambda i: (0, 0)), row_spec, _mod_spec(seq_base, tps),
                 vec_spec, vec_spec]
    args += [w_bf, x2d, mods, ln_g.reshape(1, D_MODEL), ln_b.reshape(1, D_MODEL)]
    return pl.pallas_call(
        functools.partial(_outproj_kernel, mlstm=mlstm),
        grid=(rows // tm,),
        in_specs=in_specs,
        out_specs=row_spec,
        out_shape=jax.ShapeDtypeStruct((rows, D_MODEL), _F32),
        compiler_params=_params(("arbitrary",)),
        name="ml_outproj_ln" if mlstm else "da_outproj_ln",
    )(*args)


def _halo_specs(rows, tm):
    assert tm % _HALO == 0 and rows % tm == 0
    per = tm // _HALO
    nblk = rows // _HALO
    main = pl.BlockSpec((tm, D_MODEL), lambda i, *_: (i, 0))
    prev = pl.BlockSpec((_HALO, D_MODEL), lambda i, *_: (jnp.maximum(i * per - 1, 0), 0))
    nxt = pl.BlockSpec((_HALO, D_MODEL), lambda i, *_: (jnp.minimum((i + 1) * per, nblk - 1), 0))
    return prev, main, nxt


def _modulated_ext(prev_ref, main_ref, next_ref, shift, scale, seq_len):
    tm = main_ref.shape[0]
    tiles = seq_len // tm
    t = pl.program_id(0) % tiles
    keep_prev = jnp.where(t != 0, 1.0, 0.0)
    keep_next = jnp.where(t != tiles - 1, 1.0, 0.0)
    mod = lambda r: r * (1.0 + scale) + shift
    ext = jnp.concatenate([mod(prev_ref[...]) * keep_prev, mod(main_ref[...]), mod(next_ref[...]) * keep_next],
                          axis=0)
    return ext.astype(_BF)


def _conv3(u, w, b, tm):
    rows = u.shape[0]
    up = pltpu.roll(u, 1, 0)[_HALO:_HALO + tm]
    un = pltpu.roll(u, rows - 1, 0)[_HALO:_HALO + tm]
    return up * w[0:1, :] + u[_HALO:_HALO + tm] * w[1:2, :] + un * w[2:3, :] + b


def _resident(shape):
    return pl.BlockSpec(shape, lambda i: (0,) * len(shape), pipeline_mode=pl.Buffered(1))


def _ffn_kernel(prev_ref, x_ref, next_ref, m_ref, wu_ref, cw_ref, cb_ref, wd_ref, g_ref, b_ref, o_ref,
                a_ref, *, seq_len):
    tm = x_ref.shape[0]
    h = _modulated_ext(prev_ref, x_ref, next_ref, m_ref[0, 3:4, :], m_ref[0, 4:5, :], seq_len)
    for j in range(D_FF // _FF_CHUNK):
        g0, v0 = j * _FF_CHUNK, D_FF + j * _FF_CHUNK
        gate = _conv3(_dot(h, wu_ref[:, g0:g0 + _FF_CHUNK]), cw_ref[:, g0:g0 + _FF_CHUNK],
                      cb_ref[:, g0:g0 + _FF_CHUNK], tm)
        val = _conv3(_dot(h, wu_ref[:, v0:v0 + _FF_CHUNK]), cw_ref[:, v0:v0 + _FF_CHUNK],
                     cb_ref[:, v0:v0 + _FF_CHUNK], tm)
        a_ref[:, g0:g0 + _FF_CHUNK] = (_silu(gate) * val).astype(_BF)
    z = ALPHA * x_ref[...] + m_ref[0, 5:6, :] * _dot(a_ref[...], wd_ref[...])
    o_ref[...] = _layer_norm_rows(z, g_ref[...], b_ref[...])


def _conv_ffn(x2d, mods, w_up_bf, conv_w, conv_b, w_down_bf, ln_g, ln_b, *, seq_len, seq_base, tm):
    rows = x2d.shape[0]
    tps = seq_len // tm if seq_base else None
    prev, main, nxt = _halo_specs(rows, tm)
    vec_spec = pl.BlockSpec((1, D_MODEL), lambda i: (0, 0))
    return pl.pallas_call(
        functools.partial(_ffn_kernel, seq_len=seq_len),
        grid=(rows // tm,),
        in_specs=[
            prev, main, nxt, _mod_spec(seq_base, tps),
            _resident((D_MODEL, 2 * D_FF)), _resident((3, 2 * D_FF)), _resident((1, 2 * D_FF)),
            _resident((D_FF, D_MODEL)), vec_spec, vec_spec,
        ],
        out_specs=pl.BlockSpec((tm, D_MODEL), lambda i: (i, 0)),
        out_shape=jax.ShapeDtypeStruct((rows, D_MODEL), _F32),
        scratch_shapes=[pltpu.VMEM((tm, D_FF), _BF)],
        compiler_params=_params(("arbitrary",)),
        name="conv_ffn_ln",
    )(x2d, x2d, x2d, mods, w_up_bf, conv_w, conv_b.reshape(1, 2 * D_FF), w_down_bf,
      ln_g.reshape(1, D_MODEL), ln_b.reshape(1, D_MODEL))


def _mlproj_kernel(prev_ref, x_ref, next_ref, m_ref, w_ref, wg_ref, wgt_ref, cw_ref, cb_ref, bg_ref, bgt_ref,
                   q_ref, k_ref, v_ref, og_ref, gt_ref, gtt_ref, *, seq_len):
    tm = x_ref.shape[0]
    shift, scale = m_ref[0, 0:1, :], m_ref[0, 1:2, :]
    h = _modulated_ext(prev_ref, x_ref, next_ref, shift, scale, seq_len)
    qk = _silu(_conv3(_dot(h, w_ref[:, 0:2 * ML_QK]), cw_ref[...], cb_ref[...], tm))
    q_ref[...] = qk[:, 0:ML_QK].astype(_BF)
    k_ref[...] = (qk[:, ML_QK:2 * ML_QK] * (ML_DK ** -0.5)).astype(_BF)
    hm = h[_HALO:_HALO + tm, :]
    v_ref[...] = _dot(hm, w_ref[:, 2 * ML_QK:2 * ML_QK + ML_V]).astype(_BF)
    og_ref[...] = _sigmoid(_dot(hm, w_ref[:, 2 * ML_QK + ML_V:2 * ML_QK + 2 * ML_V])).astype(_BF)
    hf = x_ref[...] * (1.0 + scale) + shift
    gt_ref[...] = _dot_f32(hf, wg_ref[...]) + bg_ref[...]
    gtt_ref[...] = _dot_nt_f32(wgt_ref[...], hf) + bgt_ref[...]


def _ml_proj(x2d, mods, w_main_bf, w_gate, conv_w, conv_b, b_gate, *, seq_len, seq_base, tm):
    rows = x2d.shape[0]
    tps = seq_len // tm if seq_base else None
    prev, main, nxt = _halo_specs(rows, tm)
    n_main = 2 * ML_QK + 2 * ML_V
    full = lambda shape: pl.BlockSpec(shape, lambda i: (0,) * len(shape))
    row = lambda n: pl.BlockSpec((tm, n), lambda i: (i, 0))
    return pl.pallas_call(
        functools.partial(_mlproj_kernel, seq_len=seq_len),
        grid=(rows // tm,),
        in_specs=[prev, main, nxt, _mod_spec(seq_base, tps), full((D_MODEL, n_main)),
                  full((D_MODEL, ML_GATES)), full((ML_GATES, D_MODEL)), full((3, 2 * ML_QK)),
                  full((1, 2 * ML_QK)), full((1, ML_GATES)), full((ML_GATES, 1))],
        out_specs=[row(ML_QK), row(ML_QK), row(ML_V), row(ML_V), row(ML_GATES),
                   pl.BlockSpec((ML_GATES, tm), lambda i: (0, i))],
        out_shape=[jax.ShapeDtypeStruct((rows, ML_QK), _BF), jax.ShapeDtypeStruct((rows, ML_QK), _BF),
                   jax.ShapeDtypeStruct((rows, ML_V), _BF), jax.ShapeDtypeStruct((rows, ML_V), _BF),
                   jax.ShapeDtypeStruct((rows, ML_GATES), _F32),
                   jax.ShapeDtypeStruct((ML_GATES, rows), _F32)],
        compiler_params=_params(("arbitrary",)),
        name="ml_inproj_conv",
    )(x2d, x2d, x2d, mods, w_main_bf, w_gate, w_gate.T, conv_w, conv_b.reshape(1, 2 * ML_QK),
      b_gate.reshape(1, ML_GATES), b_gate.reshape(ML_GATES, 1))


def _mlstm_chunk(q, k, v, i_col, i_row, b_col, b_row, total, m_prev, c_prev, n_prev, causal):
    t = q.shape[0]
    ti = lax.broadcasted_iota(jnp.int32, (t, t), 0)
    si = lax.broadcasted_iota(jnp.int32, (t, t), 1)
    mask = (si <= ti) if causal else (si >= ti)
    dmat = jnp.where(mask, b_col - b_row + i_row, _NEG)
    inter = b_col + m_prev
    m_row = jnp.maximum(inter, jnp.max(dmat, axis=1, keepdims=True))
    w_intra = jnp.exp(dmat - m_row)
    w_inter = jnp.exp(inter - m_row)
    s = _dot_nt(q, k) * w_intra
    num = w_inter * _dot(q, c_prev.astype(_BF)) + _dot(s.astype(_BF), v)
    qn = jnp.sum(q.astype(_F32) * n_prev, axis=1, keepdims=True)
    den = w_inter * qn + jnp.sum(s, axis=1, keepdims=True)
    h = num / jnp.maximum(jnp.abs(den), jnp.exp(-m_row))
    dec = total - b_col + i_col
    m_new = jnp.maximum(total + m_prev, jnp.max(dec, axis=0, keepdims=True))
    w_s = jnp.exp(dec - m_new)
    carry = jnp.exp(total + m_prev - m_new)
    kw = k.astype(_F32) * w_s
    c_new = carry * c_prev + _dot(kw.T.astype(_BF), v)
    n_new = carry * n_prev + jnp.sum(kw, axis=0, keepdims=True)
    return h, c_new, n_new, m_new


def _mlstm_kernel(*refs, has_init, want_state):
    refs = list(refs)
    (qf_ref, kf_ref, vf_ref, gf_ref, gtf_ref, qb_ref, kb_ref, vb_ref, gb_ref, gtb_ref) = refs[:10]
    pos = 10
    if has_init:
        c0_ref, n0_ref, m0_ref = refs[pos:pos + 3]
        pos += 3
    hf_ref, hb_ref = refs[pos:pos + 2]
    pos += 2
    if want_state:
        co_ref, no_ref, mo_ref = refs[pos:pos + 3]
        pos += 3
    c_ref, n_ref, m_ref = refs[pos:pos + 3]
    ci = pl.program_id(1)
    t = qf_ref.shape[1]

    @pl.when(ci == 0)
    def _():
        if has_init:
            c_ref[...] = c0_ref[0]
            n_ref[...] = n0_ref[0]
            m_ref[...] = jnp.broadcast_to(m0_ref[0], m_ref.shape)
        else:
            c_ref[...] = jnp.zeros(c_ref.shape, _F32)
            n_ref[...] = jnp.zeros(n_ref.shape, _F32)
            m_ref[...] = jnp.zeros(m_ref.shape, _F32)

    ri = lax.broadcasted_iota(jnp.int32, (t, t), 0)
    cj = lax.broadcasted_iota(jnp.int32, (t, t), 1)
    lower = (cj <= ri).astype(_BF)
    upper = (cj >= ri).astype(_BF)

    def cum(mat, x):
        return sum(_dot(mat, p) for p in _split3(x))

    def cum_t(x, mat):
        return sum(_dot(p, mat) for p in _split3(x))

    for d, (q_ref, k_ref, v_ref, g_ref, gt_ref, h_ref) in enumerate(
            ((qf_ref, kf_ref, vf_ref, gf_ref, gtf_ref, hf_ref),
             (qb_ref, kb_ref, vb_ref, gb_ref, gtb_ref, hb_ref))):
        g = g_ref[0]
        gt = gt_ref[...]
        is_f_col = (lax.broadcasted_iota(jnp.int32, (1, ML_GATES), 1) % 8) >= 4
        is_f_row = (lax.broadcasted_iota(jnp.int32, (ML_GATES, 1), 0) % 8) >= 4
        x = jnp.where(is_f_col, _log_sigmoid(g), g)
        xt = jnp.where(is_f_row, _log_sigmoid(gt), gt)
        if d == 0:
            bc = cum(lower, x)
            br = cum_t(xt, upper)
        else:
            bc = cum(upper, x)
            br = cum_t(xt, lower)
        tot = jnp.sum(x, axis=0, keepdims=True)
        for hd in range(ML_HEADS):
            ic, fc = d * 8 + hd, d * 8 + 4 + hd
            st = d * ML_HEADS + hd
            h, c_new, n_new, m_new = _mlstm_chunk(
                q_ref[0, :, hd * ML_DK:(hd + 1) * ML_DK], k_ref[0, :, hd * ML_DK:(hd + 1) * ML_DK],
                v_ref[0, :, hd * ML_DV:(hd + 1) * ML_DV],
                x[:, ic:ic + 1], xt[ic:ic + 1, :], bc[:, fc:fc + 1], br[fc:fc + 1, :],
                tot[:, fc:fc + 1], m_ref[st:st + 1, 0:1], c_ref[st], n_ref[st:st + 1, :], d == 0)
            h_ref[0, :, hd * ML_DV:(hd + 1) * ML_DV] = h
            c_ref[st] = c_new
            n_ref[st:st + 1, :] = n_new
            m_ref[st:st + 1, :] = jnp.broadcast_to(m_new, (1, 128))

    if want_state:
        @pl.when(ci == pl.num_programs(1) - 1)
        def _():
            co_ref[0] = c_ref[...]
            no_ref[0] = n_ref[...]
            mo_ref[0] = m_ref[...]


def _mlstm_scan(q, k, v, gates, gates_t, init, *, want_state):
    b, l, _ = q.shape
    t = _ML_CHUNK
    nc = l // t
    fwd = lambda n: pl.BlockSpec((1, t, n), lambda bi, ci: (bi, ci, 0))
    bwd = lambda n: pl.BlockSpec((1, t, n), lambda bi, ci: (bi, nc - 1 - ci, 0))
    gtf = pl.BlockSpec((ML_GATES, t), lambda bi, ci: (0, bi * nc + ci))
    gtb = pl.BlockSpec((ML_GATES, t), lambda bi, ci: (0, bi * nc + nc - 1 - ci))
    in_specs = [fwd(ML_QK), fwd(ML_QK), fwd(ML_V), fwd(ML_GATES), gtf,
                bwd(ML_QK), bwd(ML_QK), bwd(ML_V), bwd(ML_GATES), gtb]
    args = [q, k, v, gates, gates_t, q, k, v, gates, gates_t]
    has_init = init is not None
    if has_init:
        c0, n0, m0 = init
        in_specs += [pl.BlockSpec((1, 8, ML_DK, ML_DV), lambda bi, ci: (bi, 0, 0, 0)),
                     pl.BlockSpec((1, 8, ML_DK), lambda bi, ci: (bi, 0, 0)),
                     pl.BlockSpec((1, 8, 1), lambda bi, ci: (bi, 0, 0))]
        args += [c0.reshape(b, 8, ML_DK, ML_DV), n0.reshape(b, 8, ML_DK), m0.reshape(b, 8, 1)]
    out_specs = [fwd(ML_V), bwd(ML_V)]
    out_shape = [jax.ShapeDtypeStruct((b, l, ML_V), _F32), jax.ShapeDtypeStruct((b, l, ML_V), _F32)]
    if want_state:
        out_specs += [pl.BlockSpec((1, 8, ML_DK, ML_DV), lambda bi, ci: (bi, 0, 0, 0)),
                      pl.BlockSpec((1, 8, ML_DK), lambda bi, ci: (bi, 0, 0)),
                      pl.BlockSpec((1, 8, 128), lambda bi, ci: (bi, 0, 0))]
        out_shape += [jax.ShapeDtypeStruct((b, 8, ML_DK, ML_DV), _F32),
                      jax.ShapeDtypeStruct((b, 8, ML_DK), _F32),
                      jax.ShapeDtypeStruct((b, 8, 128), _F32)]
    return pl.pallas_call(
        functools.partial(_mlstm_kernel, has_init=has_init, want_state=want_state),
        grid=(b, nc),
        in_specs=in_specs,
        out_specs=out_specs,
        out_shape=out_shape,
        scratch_shapes=[pltpu.VMEM((8, ML_DK, ML_DV), _F32), pltpu.VMEM((8, ML_DK), _F32),
                        pltpu.VMEM((8, 128), _F32)],
        compiler_params=_params(("arbitrary", "arbitrary")),
        name="mlstm_scan_state" if want_state else "mlstm_scan",
    )(*args)


def kernel(x_prompt, x_sample, c, cache_k, cache_v, state_C, state_n, state_m, c_ctx, ada_w, ada_b, ln_g, ln_b,
           da_w_qkv, da_lam, da_subln, da_w_o, ml_w_in, ml_conv_w, ml_conv_b, ml_b_gate, ml_norm_w, ml_w_out,
           ffn_w_up, ffn_conv_w, ffn_conv_b, ffn_w_down):
    bp, lp, _ = x_prompt.shape
    bs, ls, _ = x_sample.shape
    cond = jnp.concatenate([c_ctx[None, :], c, jnp.zeros((8 - 1 - bs, D_MODEL), _F32)], axis=0)
    mods = _modulation(cond, ada_w, ada_b).reshape(DEPTH, 8, 6, D_MODEL)

    xp = x_prompt.reshape(bp * lp, D_MODEL)
    xs = x_sample.reshape(bs * ls, D_MODEL)
    groups = (dict(seq_len=lp, seq_base=0), dict(seq_len=ls, seq_base=1))

    lam_init = 0.8 - 0.6 * math.exp(-0.3 * 0)
    w_qkv = da_w_qkv[0].astype(_BF)
    w_o = da_w_o[0].astype(_BF)
    qp, kp, vp = _qkv_proj(xp, mods[0], w_qkv, rope=False, kv_dtype=_F32, tm=_ROW_TILE, **groups[0])
    qs, ks, vs = _qkv_proj(xs, mods[0], w_qkv, rope=True, kv_dtype=_BF, tm=_ROW_TILE, **groups[1])
    as3 = lambda a, b: a.reshape(b, -1, D_MODEL)
    op = _diff_attention(as3(qp, bp), as3(kp, bp), as3(vp, bp), None, None, da_lam[0], da_subln[0],
                         lam_init=lam_init, tq=256, tk=512)
    os_ = _diff_attention(as3(qs, bs), cache_k[:, 0].reshape(bs, -1, D_MODEL),
                          cache_v[:, 0].reshape(bs, -1, D_MODEL), as3(ks, bs), as3(vs, bs),
                          da_lam[0], da_subln[0], lam_init=lam_init, tq=256, tk=512)
    xp = _outproj([op.reshape(-1, D_MODEL)], w_o, xp, mods[0], ln_g[0, 0], ln_b[0, 0], tm=_ROW_TILE, **groups[0])
    xs = _outproj([os_.reshape(-1, D_MODEL)], w_o, xs, mods[0], ln_g[0, 0], ln_b[0, 0], tm=_ROW_TILE, **groups[1])
    new_k = kp.reshape(bp, 1, lp, DA_HEADS, 2, DA_DK)
    new_v = vp.reshape(bp, 1, lp, DA_HEADS, DA_DV)

    def ffn(x2d, i, grp):
        return _conv_ffn(x2d, mods[i], ffn_w_up[i].astype(_BF), ffn_conv_w[i], ffn_conv_b[i],
                         ffn_w_down[i].astype(_BF), ln_g[i, 1], ln_b[i, 1], tm=min(_ROW_TILE, grp["seq_len"]), **grp)

    xp = ffn(xp, 0, groups[0])
    xs = ffn(xs, 0, groups[1])

    n_main = 2 * ML_QK + 2 * ML_V
    w_main = ml_w_in[0][:, :n_main].astype(_BF)
    w_gate = ml_w_in[0][:, n_main:]
    w_out = ml_w_out[0].astype(_BF)
    outs = []
    for x2d, grp, nb, init in ((xp, groups[0], bp, None),
                               (xs, groups[1], bs, (state_C[:, 0], state_n[:, 0], state_m[:, 0]))):
        q, k, v, og, gt, gtt = _ml_proj(x2d, mods[1], w_main, w_gate, ml_conv_w[0], ml_conv_b[0], ml_b_gate[0],
                                        tm=min(_ROW_TILE, grp["seq_len"]), **grp)
        r3 = lambda a: a.reshape(nb, -1, a.shape[-1])
        res = _mlstm_scan(r3(q), r3(k), r3(v), r3(gt), gtt, init, want_state=init is None)
        hf, hb = res[0].reshape(-1, ML_V), res[1].reshape(-1, ML_V)
        x2d = _outproj([hf, hb, og], w_out, x2d, mods[1], ln_g[1, 0], ln_b[1, 0], tm=_ROW_TILE,
                       norm_w=ml_norm_w[0], **grp)
        outs.append((ffn(x2d, 1, grp), res[2:]))
    (xp, (c_fin, n_fin, m_fin)), (xs, _) = outs
    new_c = c_fin.reshape(bp, 1, 2, ML_HEADS, ML_DK, ML_DV)
    new_n = n_fin.reshape(bp, 1, 2, ML_HEADS, ML_DK)
    new_m = m_fin[:, :, 0].reshape(bp, 1, 2, ML_HEADS)
    return (xp.reshape(bp, lp, D_MODEL), xs.reshape(bs, ls, D_MODEL), new_k, new_v, new_c, new_n, new_m)
```

```python
import functools
import math

import jax
import jax.numpy as jnp
from jax import lax
from jax.experimental import pallas as pl
from jax.experimental.pallas import tpu as pltpu

D_MODEL = 1024
DEPTH = 2
GRID_W = 64
DA_HEADS = 8
DA_DK = 64
DA_DV = 128
ML_HEADS = 4
ML_DK = 128
ML_DV = 256
ML_QK = ML_HEADS * ML_DK
ML_V = ML_HEADS * ML_DV
ML_GATES = 16
D_FF = 2816
ROPE_THETA = 10000.0
ALPHA = (2 * DEPTH) ** 0.25
EPS = 1e-5

_BF = jnp.bfloat16
_F32 = jnp.float32
_NEG = -1e30

_VMEM_LIMIT_BYTES = 56 * 1024 * 1024
_HALO = 16
_ML_CHUNK = 256
_ROW_TILE = 512
_FF_CHUNK = 256
_Q_SCALE = DA_DK ** -0.5 * math.log2(math.e)


def _dot(a, b):
    return jnp.dot(a, b, preferred_element_type=_F32)


def _dot_nt(a, b):
    return lax.dot_general(a, b, (((1,), (1,)), ((), ())), preferred_element_type=_F32)


def _split2(x):
    hi = x.astype(_BF)
    lo = (x - hi.astype(_F32)).astype(_BF)
    return hi, lo


def _split3(x):
    hi = x.astype(_BF)
    r = x - hi.astype(_F32)
    mid = r.astype(_BF)
    lo = (r - mid.astype(_F32)).astype(_BF)
    return hi, mid, lo


def _dot_f32(a, b):
    ah, al = _split2(a)
    bh, bl = _split2(b)
    return _dot(ah, bh) + _dot(al, bh) + _dot(ah, bl)


def _dot_nt_f32(a, b):
    ah, al = _split2(a)
    bh, bl = _split2(b)
    return _dot_nt(ah, bh) + _dot_nt(al, bh) + _dot_nt(ah, bl)


def _sigmoid(x):
    return 1.0 / (1.0 + jnp.exp(-x))


def _silu(x):
    return x * _sigmoid(x)


def _log_sigmoid(x):
    return jnp.minimum(x, 0.0) - jnp.log(1.0 + jnp.exp(-jnp.abs(x)))


def _layer_norm_rows(z, g, b):
    mu = jnp.mean(z, axis=-1, keepdims=True)
    zc = z - mu
    var = jnp.mean(zc * zc, axis=-1, keepdims=True)
    return zc * lax.rsqrt(var + EPS) * g + b


def _params(sem):
    return pltpu.CompilerParams(dimension_semantics=sem, vmem_limit_bytes=_VMEM_LIMIT_BYTES)


def _mod_kernel(c_ref, w_ref, b_ref, o_ref):
    s = _silu(c_ref[...])
    o_ref[0] = _dot_f32(s, w_ref[0]) + b_ref[0]


def _modulation(cond, ada_w, ada_b):
    tn = 1024
    n = 6 * D_MODEL
    return pl.pallas_call(
        _mod_kernel,
        grid=(DEPTH, n // tn),
        in_specs=[
            pl.BlockSpec((8, D_MODEL), lambda l, j: (0, 0)),
            pl.BlockSpec((1, D_MODEL, tn), lambda l, j: (l, 0, j)),
            pl.BlockSpec((1, 1, tn), lambda l, j: (l, 0, j)),
        ],
        out_specs=pl.BlockSpec((1, 8, tn), lambda l, j: (l, 0, j)),
        out_shape=jax.ShapeDtypeStruct((DEPTH, 8, n), _F32),
        compiler_params=_params(("arbitrary", "arbitrary")),
        name="adaln_mod",
    )(cond, ada_w, ada_b.reshape(DEPTH, 1, n))


def _mod_spec(seq_base, tiles_per_seq):
    if tiles_per_seq is None:
        return pl.BlockSpec((1, 6, D_MODEL), lambda i, *_: (seq_base, 0, 0))
    return pl.BlockSpec((1, 6, D_MODEL), lambda i, *_: (seq_base + i // tiles_per_seq, 0, 0))


def _qkv_kernel(*refs, rope):
    if rope:
        x_ref, m_ref, w_ref, cos_ref, sa_ref, sb_ref, q_ref, k_ref, v_ref = refs
    else:
        x_ref, m_ref, w_ref, q_ref, k_ref, v_ref = refs
    h = (x_ref[...] * (1.0 + m_ref[0, 1:2, :]) + m_ref[0, 0:1, :]).astype(_BF)
    for c, o_ref in enumerate((q_ref, k_ref, v_ref)):
        y = _dot(h, w_ref[:, c * D_MODEL:(c + 1) * D_MODEL])
        if rope and c < 2:
            cos, sa, sb = cos_ref[...], sa_ref[...], sb_ref[...]
            for hd in range(DA_HEADS):
                yh = y[:, hd * 128:(hd + 1) * 128]
                yh = yh * cos + pltpu.roll(yh, 112, 1) * sa + pltpu.roll(yh, 16, 1) * sb
                if c == 0:
                    yh = yh * _Q_SCALE
                o_ref[:, hd * 128:(hd + 1) * 128] = yh.astype(o_ref.dtype)
        else:
            if c == 0:
                y = y * _Q_SCALE
            o_ref[...] = y.astype(o_ref.dtype)


def _rope_tables(seq_len):
    t = jnp.arange(seq_len)
    row = (t // GRID_W).astype(_F32)
    col = (t % GRID_W).astype(_F32)
    lane = jnp.arange(128)
    d = lane % DA_DK
    half = DA_DK // 2
    nf = half // 2
    dd = d % half
    f = dd % nf
    odd = (dd // nf) == 1
    inv = ROPE_THETA ** (-jnp.arange(nf, dtype=_F32) / nf)
    pos = jnp.where((d < half)[None, :], row[:, None], col[:, None])
    ang = pos * inv[f][None, :]
    cos, sin = jnp.cos(ang), jnp.sin(ang)
    sa = jnp.where(odd[None, :], 0.0, -sin)
    sb = jnp.where(odd[None, :], sin, 0.0)
    return cos, sa, sb


def _qkv_proj(x2d, mods, w_bf, *, seq_len, seq_base, rope, kv_dtype, tm):
    rows = x2d.shape[0]
    tps = seq_len // tm if seq_base else None
    in_specs = [
        pl.BlockSpec((tm, D_MODEL), lambda i: (i, 0)),
        _mod_spec(seq_base, tps),
        pl.BlockSpec((D_MODEL, 3 * D_MODEL), lambda i: (0, 0)),
    ]
    args = [x2d, mods, w_bf]
    if rope:
        tabs = _rope_tables(seq_len)
        in_specs += [pl.BlockSpec((tm, 128), lambda i: (i % (seq_len // tm), 0))] * 3
        args += list(tabs)
    out_spec = pl.BlockSpec((tm, D_MODEL), lambda i: (i, 0))
    return pl.pallas_call(
        functools.partial(_qkv_kernel, rope=rope),
        grid=(rows // tm,),
        in_specs=in_specs,
        out_specs=[out_spec, out_spec, out_spec],
        out_shape=[jax.ShapeDtypeStruct((rows, D_MODEL), _BF),
                   jax.ShapeDtypeStruct((rows, D_MODEL), kv_dtype),
                   jax.ShapeDtypeStruct((rows, D_MODEL), kv_dtype)],
        compiler_params=_params(("arbitrary",)),
        name="da_qkv_rope" if rope else "da_qkv",
    )(*args)


def _group_rows(x):
    return x.reshape(x.shape[0] // 8, 8, x.shape[1])


def _attn_kernel(*refs, n_new, tk, lam_init, heads):
    if n_new:
        q_ref, ka_ref, va_ref, kb_ref, vb_ref, lam_ref, sub_ref, o_ref, vta_ref, vtb_ref, s_ref = refs
    else:
        q_ref, ka_ref, va_ref, lam_ref, sub_ref, o_ref, vta_ref, s_ref = refs
    la = ka_ref.shape[1]
    lanes = lambda hd: slice(hd * 128, (hd + 1) * 128)

    @pl.when(pl.program_id(2) == 0)
    def _():
        for hd in range(heads):
            vta_ref[hd] = va_ref[0, :, lanes(hd)].astype(_F32).T.astype(_BF)
            for j in range(n_new):
                vtb_ref[hd, j] = vb_ref[0, j * tk:(j + 1) * tk, lanes(hd)].astype(_F32).T.astype(_BF)

    lf = lam_ref[...]
    lam = (jnp.exp(jnp.sum(lf[0:1] * lf[1:2], axis=1, keepdims=True))
           - jnp.exp(jnp.sum(lf[2:3] * lf[3:4], axis=1, keepdims=True)) + lam_init)

    for hd in range(heads):
        q = q_ref[0, :, lanes(hd)]
        first_map = lax.broadcasted_iota(jnp.int32, q.shape, 1) < DA_DK
        qm = (jnp.where(first_map, q, jnp.zeros_like(q)), jnp.where(first_map, jnp.zeros_like(q), q))

        tiles = [(0, la, lambda: ka_ref[0, :, lanes(hd)].astype(_BF), lambda: vta_ref[hd])]
        for j in range(n_new):
            tiles.append((la + j * tk, tk, lambda j=j: kb_ref[0, j * tk:(j + 1) * tk, lanes(hd)],
                          lambda j=j: vtb_ref[hd, j]))

        def scores(mp, tile, mrun):
            off, rows, load_k, _ = tile
            s = _dot_nt(load_k(), qm[mp])
            s_ref[hd, mp, off:off + rows, :] = s
            m = jnp.max(_group_rows(s), axis=0)
            return m if mrun is None else jnp.maximum(mrun, m)

        def values(mp, tile, mx, state):
            off, rows, _, load_vt = tile
            e = jnp.exp2(s_ref[hd, mp, off:off + rows, :] - mx)
            pv = _dot(load_vt(), e.astype(_BF))
            ls = jnp.sum(_group_rows(e), axis=0)
            return (pv, ls) if state is None else (state[0] + pv, state[1] + ls)

        m0 = m1 = st0 = st1 = None
        for tl in tiles:
            m0 = scores(0, tl, m0)
            m1 = scores(1, tl, m1)
        mx0 = jnp.max(m0, axis=0, keepdims=True)
        mx1 = jnp.max(m1, axis=0, keepdims=True)
        for tl in tiles:
            st0 = values(0, tl, mx0, st0)
            st1 = values(1, tl, mx1, st1)
        l0 = jnp.sum(st0[1], axis=0, keepdims=True)
        l1 = jnp.sum(st1[1], axis=0, keepdims=True)

        o = st0[0] * (1.0 / l0) - st1[0] * (lam / l1)
        ms = jnp.mean(o * o, axis=0, keepdims=True)
        o = o * lax.rsqrt(ms + EPS) * (sub_ref[...] * (1.0 - lam_init))
        o_ref[0, :, lanes(hd)] = o.T.astype(o_ref.dtype)


def _diff_attention(q, ka, va, kb, vb, lam, subln, *, lam_init, tq, tk, heads):
    b, lq, _ = q.shape
    la = ka.shape[1]
    n_new = 0 if kb is None else kb.shape[1] // tk
    width = 128 * heads
    hspec = lambda rows: pl.BlockSpec((1, rows, width), lambda bi, hi, qi: (bi, 0, hi))
    in_specs = [pl.BlockSpec((1, tq, width), lambda bi, hi, qi: (bi, qi, hi)), hspec(la), hspec(la)]
    args = [q, ka, va]
    scratch = [pltpu.VMEM((heads, 128, la), _BF)]
    if n_new:
        in_specs += [hspec(kb.shape[1]), hspec(kb.shape[1])]
        args += [kb, vb]
        scratch.append(pltpu.VMEM((heads, n_new, 128, tk), _BF))
    in_specs += [pl.BlockSpec((4, DA_DK), lambda bi, hi, qi: (0, 0)),
                 pl.BlockSpec((DA_DV, 1), lambda bi, hi, qi: (0, 0))]
    args += [lam, subln.reshape(DA_DV, 1)]
    lk = la + (kb.shape[1] if n_new else 0)
    scratch.append(pltpu.VMEM((heads, 2, lk, tq), _F32))
    return pl.pallas_call(
        functools.partial(_attn_kernel, n_new=n_new, tk=tk, lam_init=lam_init, heads=heads),
        grid=(b, DA_HEADS // heads, lq // tq),
        in_specs=in_specs,
        out_specs=pl.BlockSpec((1, tq, width), lambda bi, hi, qi: (bi, qi, hi)),
        out_shape=jax.ShapeDtypeStruct((b, lq, D_MODEL), _BF),
        scratch_shapes=scratch,
        compiler_params=_params(("arbitrary", "arbitrary", "arbitrary")),
        name="diff_attn_ctx" if n_new else "diff_attn",
    )(*args)


def _outproj_kernel(*refs, mlstm):
    if mlstm:
        hf_ref, hb_ref, og_ref, nw_ref, w_ref, x_ref, m_ref, g_ref, b_ref, o_ref = refs
        h = hf_ref[...] + hb_ref[...]
        parts = []
        for hd in range(ML_HEADS):
            hh = h[:, hd * ML_DV:(hd + 1) * ML_DV]
            mu = jnp.mean(hh, axis=-1, keepdims=True)
            hc = hh - mu
            var = jnp.mean(hc * hc, axis=-1, keepdims=True)
            parts.append(hc * lax.rsqrt(var + EPS))
        hn = jnp.concatenate(parts, axis=-1) * nw_ref[...]
        a = (og_ref[...].astype(_F32) * hn).astype(_BF)
    else:
        a_ref, w_ref, x_ref, m_ref, g_ref, b_ref, o_ref = refs
        a = a_ref[...]
    y = _dot(a, w_ref[...])
    z = ALPHA * x_ref[...] + m_ref[0, 2:3, :] * y
    o_ref[...] = _layer_norm_rows(z, g_ref[...], b_ref[...])


def _outproj(acts, w_bf, x2d, mods, ln_g, ln_b, *, seq_len, seq_base, tm, norm_w=None):
    rows = x2d.shape[0]
    mlstm = norm_w is not None
    tps = seq_len // tm if seq_base else None
    row_spec = pl.BlockSpec((tm, D_MODEL), lambda i: (i, 0))
    vec_spec = pl.BlockSpec((1, D_MODEL), lambda i: (0, 0))
    in_specs = [row_spec] * len(acts)
    args = list(acts)
    if mlstm:
        in_specs.append(vec_spec)
        args.append(norm_w.reshape(1, D_MODEL))
    in_specs += [pl.BlockSpec((D_MODEL, D_MODEL), lambda i: (0, 0)), row_spec, _mod_spec(seq_base, tps),
                 vec_spec, vec_spec]
    args += [w_bf, x2d, mods, ln_g.reshape(1, D_MODEL), ln_b.reshape(1, D_MODEL)]
    return pl.pallas_call(
        functools.partial(_outproj_kernel, mlstm=mlstm),
        grid=(rows // tm,),
        in_specs=in_specs,
        out_specs=row_spec,
        out_shape=jax.ShapeDtypeStruct((rows, D_MODEL), _F32),
        compiler_params=_params(("arbitrary",)),
        name="ml_outproj_ln" if mlstm else "da_outproj_ln",
    )(*args)


def _halo_specs(rows, tm):
    assert tm % _HALO == 0 and rows % tm == 0
    per = tm // _HALO
    nblk = rows // _HALO
    main = pl.BlockSpec((tm, D_MODEL), lambda i, *_: (i, 0))
    prev = pl.BlockSpec((_HALO, D_MODEL), lambda i, *_: (jnp.maximum(i * per - 1, 0), 0))
    nxt = pl.BlockSpec((_HALO, D_MODEL), lambda i, *_: (jnp.minimum((i + 1) * per, nblk - 1), 0))
    return prev, main, nxt


def _modulated_ext(prev_ref, main_ref, next_ref, shift, scale, seq_len):
    tm = main_ref.shape[0]
    tiles = seq_len // tm
    t = pl.program_id(0) % tiles
    keep_prev = jnp.where(t != 0, 1.0, 0.0)
    keep_next = jnp.where(t != tiles - 1, 1.0, 0.0)
    mod = lambda r: r * (1.0 + scale) + shift
    ext = jnp.concatenate([mod(prev_ref[...]) * keep_prev, mod(main_ref[...]), mod(next_ref[...]) * keep_next],
                          axis=0)
    return ext.astype(_BF)


def _conv3(u, w, b, tm):
    rows = u.shape[0]
    up = pltpu.roll(u, 1, 0)[_HALO:_HALO + tm]
    un = pltpu.roll(u, rows - 1, 0)[_HALO:_HALO + tm]
    return up * w[0:1, :] + u[_HALO:_HALO + tm] * w[1:2, :] + un * w[2:3, :] + b


def _resident(shape):
    return pl.BlockSpec(shape, lambda i: (0,) * len(shape), pipeline_mode=pl.Buffered(1))


def _ffn_kernel(prev_ref, x_ref, next_ref, m_ref, wu_ref, cw_ref, cb_ref, wd_ref, g_ref, b_ref, o_ref,
                a_ref, *, seq_len):
    tm = x_ref.shape[0]
    h = _modulated_ext(prev_ref, x_ref, next_ref, m_ref[0, 3:4, :], m_ref[0, 4:5, :], seq_len)
    for j in range(D_FF // _FF_CHUNK):
        g0, v0 = j * _FF_CHUNK, D_FF + j * _FF_CHUNK
        gate = _conv3(_dot(h, wu_ref[:, g0:g0 + _FF_CHUNK]), cw_ref[:, g0:g0 + _FF_CHUNK],
                      cb_ref[:, g0:g0 + _FF_CHUNK], tm)
        val = _conv3(_dot(h, wu_ref[:, v0:v0 + _FF_CHUNK]), cw_ref[:, v0:v0 + _FF_CHUNK],
                     cb_ref[:, v0:v0 + _FF_CHUNK], tm)
        a_ref[:, g0:g0 + _FF_CHUNK] = (_silu(gate) * val).astype(_BF)
    z = ALPHA * x_ref[...] + m_ref[0, 5:6, :] * _dot(a_ref[...], wd_ref[...])
    o_ref[...] = _layer_norm_rows(z, g_ref[...], b_ref[...])


def _conv_ffn(x2d, mods, w_up_bf, conv_w, conv_b, w_down_bf, ln_g, ln_b, *, seq_len, seq_base, tm):
    rows = x2d.shape[0]
    tps = seq_len // tm if seq_base else None
    prev, main, nxt = _halo_specs(rows, tm)
    vec_spec = pl.BlockSpec((1, D_MODEL), lambda i: (0, 0))
    return pl.pallas_call(
        functools.partial(_ffn_kernel, seq_len=seq_len),
        grid=(rows // tm,),
        in_specs=[
            prev, main, nxt, _mod_spec(seq_base, tps),
            _resident((D_MODEL, 2 * D_FF)), _resident((3, 2 * D_FF)), _resident((1, 2 * D_FF)),
            _resident((D_FF, D_MODEL)), vec_spec, vec_spec,
        ],
        out_specs=pl.BlockSpec((tm, D_MODEL), lambda i: (i, 0)),
        out_shape=jax.ShapeDtypeStruct((rows, D_MODEL), _F32),
        scratch_shapes=[pltpu.VMEM((tm, D_FF), _BF)],
        compiler_params=_params(("arbitrary",)),
        name="conv_ffn_ln",
    )(x2d, x2d, x2d, mods, w_up_bf, conv_w, conv_b.reshape(1, 2 * D_FF), w_down_bf,
      ln_g.reshape(1, D_MODEL), ln_b.reshape(1, D_MODEL))


def _mlproj_kernel(prev_ref, x_ref, next_ref, m_ref, w_ref, wgt_ref, cw_ref, cb_ref, bgt_ref,
                   q_ref, k_ref, v_ref, og_ref, gtt_ref, *, seq_len):
    tm = x_ref.shape[0]
    shift, scale = m_ref[0, 0:1, :], m_ref[0, 1:2, :]
    h = _modulated_ext(prev_ref, x_ref, next_ref, shift, scale, seq_len)
    qk = _silu(_conv3(_dot(h, w_ref[:, 0:2 * ML_QK]), cw_ref[...], cb_ref[...], tm))
    q_ref[...] = qk[:, 0:ML_QK].astype(_BF)
    k_ref[...] = (qk[:, ML_QK:2 * ML_QK] * (ML_DK ** -0.5)).astype(_BF)
    hm = h[_HALO:_HALO + tm, :]
    v_ref[...] = _dot(hm, w_ref[:, 2 * ML_QK:2 * ML_QK + ML_V]).astype(_BF)
    og_ref[...] = _sigmoid(_dot(hm, w_ref[:, 2 * ML_QK + ML_V:2 * ML_QK + 2 * ML_V])).astype(_BF)
    hf = x_ref[...] * (1.0 + scale) + shift
    gtt_ref[...] = _dot_nt_f32(wgt_ref[...], hf) + bgt_ref[...]


def _ml_proj(x2d, mods, w_main_bf, w_gate, conv_w, conv_b, b_gate, *, seq_len, seq_base, tm):
    rows = x2d.shape[0]
    tps = seq_len // tm if seq_base else None
    prev, main, nxt = _halo_specs(rows, tm)
    n_main = 2 * ML_QK + 2 * ML_V
    full = lambda shape: pl.BlockSpec(shape, lambda i: (0,) * len(shape))
    row = lambda n: pl.BlockSpec((tm, n), lambda i: (i, 0))
    return pl.pallas_call(
        functools.partial(_mlproj_kernel, seq_len=seq_len),
        grid=(rows // tm,),
        in_specs=[prev, main, nxt, _mod_spec(seq_base, tps), full((D_MODEL, n_main)),
                  full((ML_GATES, D_MODEL)), full((3, 2 * ML_QK)), full((1, 2 * ML_QK)), full((ML_GATES, 1))],
        out_specs=[row(ML_QK), row(ML_QK), row(ML_V), row(ML_V),
                   pl.BlockSpec((ML_GATES, tm), lambda i: (0, i))],
        out_shape=[jax.ShapeDtypeStruct((rows, ML_QK), _BF), jax.ShapeDtypeStruct((rows, ML_QK), _BF),
                   jax.ShapeDtypeStruct((rows, ML_V), _BF), jax.ShapeDtypeStruct((rows, ML_V), _BF),
                   jax.ShapeDtypeStruct((ML_GATES, rows), _F32)],
        compiler_params=_params(("arbitrary",)),
        name="ml_inproj_conv",
    )(x2d, x2d, x2d, mods, w_main_bf, w_gate.T, conv_w, conv_b.reshape(1, 2 * ML_QK),
      b_gate.reshape(ML_GATES, 1))


def _mlstm_chunk(q, k, v, i_row, b_col, b_row, total, m_prev, c_prev, n_prev, causal):
    t = q.shape[0]
    wide = lambda col: jnp.broadcast_to(col, (t, 128))
    tile2 = lambda w: jnp.concatenate([w] * (t // 128), axis=1)
    ti = lax.broadcasted_iota(jnp.int32, (t, t), 0)
    si = lax.broadcasted_iota(jnp.int32, (t, t), 1)
    a_row = i_row - b_row
    amat = jnp.where((si <= ti) if causal else (si >= ti), a_row, _NEG)
    g_col = jnp.maximum(m_prev, jnp.max(amat, axis=1, keepdims=True))
    g_w = wide(g_col)
    w_intra = jnp.exp(amat - tile2(g_w))
    w_inter_w = jnp.exp(m_prev - g_w)
    s = _dot_nt(q, k) * w_intra
    den_parts = w_inter_w * (q.astype(_F32) * n_prev)
    for c in range(t // 128):
        den_parts = den_parts + s[:, c * 128:(c + 1) * 128]
    den_col = jnp.sum(den_parts, axis=1, keepdims=True)
    r_col = 1.0 / jnp.maximum(jnp.abs(den_col), jnp.exp(-(b_col + g_col)))
    num = jnp.concatenate([w_inter_w, w_inter_w], axis=1) * _dot(q, c_prev.astype(_BF)) + _dot(s.astype(_BF), v)
    r_w = wide(r_col)
    h = num * jnp.concatenate([r_w, r_w], axis=1)
    g_end = jnp.maximum(m_prev, jnp.max(a_row, axis=1, keepdims=True))
    w_s = jnp.exp(a_row - g_end)
    carry = jnp.exp(m_prev - g_end)
    kw_t = (k.astype(_F32).T * w_s).astype(_BF)
    c_new = carry * c_prev + _dot(kw_t, v)
    ws_hi, ws_lo = _split2(jnp.broadcast_to(w_s, (8, t)))
    n_new = carry * n_prev + (_dot(ws_hi, k) + _dot(ws_lo, k))[0:1, :]
    return h, c_new, n_new, total + g_end


def _mlstm_kernel(*refs, has_init, want_state):
    refs = list(refs)
    (qf_ref, kf_ref, vf_ref, gtf_ref, qb_ref, kb_ref, vb_ref, gtb_ref) = refs[:8]
    pos = 8
    if has_init:
        c0_ref, n0_ref, m0_ref = refs[pos:pos + 3]
        pos += 3
    hf_ref, hb_ref = refs[pos:pos + 2]
    pos += 2
    if want_state:
        co_ref, no_ref, mo_ref = refs[pos:pos + 3]
        pos += 3
    c_ref, n_ref, m_ref = refs[pos:pos + 3]
    ci = pl.program_id(1)
    t = qf_ref.shape[1]

    @pl.when(ci == 0)
    def _():
        if has_init:
            c_ref[...] = c0_ref[0]
            n_ref[...] = n0_ref[0]
            m_ref[...] = jnp.broadcast_to(m0_ref[0], m_ref.shape)
        else:
            c_ref[...] = jnp.zeros(c_ref.shape, _F32)
            n_ref[...] = jnp.zeros(n_ref.shape, _F32)
            m_ref[...] = jnp.zeros(m_ref.shape, _F32)

    ri = lax.broadcasted_iota(jnp.int32, (t, t), 0)
    cj = lax.broadcasted_iota(jnp.int32, (t, t), 1)
    lower = (cj <= ri).astype(_BF)
    upper = (cj >= ri).astype(_BF)

    def cum(mat, xt):
        return sum(_dot_nt(mat, p) for p in _split3(xt))

    def cum_t(xt, mat):
        return sum(_dot(p, mat) for p in _split3(xt))

    for d, (q_ref, k_ref, v_ref, gt_ref, h_ref) in enumerate(
            ((qf_ref, kf_ref, vf_ref, gtf_ref, hf_ref), (qb_ref, kb_ref, vb_ref, gtb_ref, hb_ref))):
        gt = gt_ref[...]
        is_f_row = (lax.broadcasted_iota(jnp.int32, (ML_GATES, 1), 0) % 8) >= 4
        xt = jnp.where(is_f_row, _log_sigmoid(gt), gt)
        if d == 0:
            bc = cum(lower, xt)
            br = cum_t(xt, upper)
        else:
            bc = cum(upper, xt)
            br = cum_t(xt, lower)
        tot = jnp.sum(xt, axis=1, keepdims=True)
        for hd in range(ML_HEADS):
            ic, fc = d * 8 + hd, d * 8 + 4 + hd
            st = d * ML_HEADS + hd
            h, c_new, n_new, m_new = _mlstm_chunk(
                q_ref[0, :, hd * ML_DK:(hd + 1) * ML_DK], k_ref[0, :, hd * ML_DK:(hd + 1) * ML_DK],
                v_ref[0, :, hd * ML_DV:(hd + 1) * ML_DV],
                xt[ic:ic + 1, :], bc[:, fc:fc + 1], br[fc:fc + 1, :],
                tot[fc:fc + 1, :], m_ref[st:st + 1, 0:1], c_ref[st], n_ref[st:st + 1, :], d == 0)
            h_ref[0, :, hd * ML_DV:(hd + 1) * ML_DV] = h
            c_ref[st] = c_new
            n_ref[st:st + 1, :] = n_new
            m_ref[st:st + 1, :] = jnp.broadcast_to(m_new, (1, 128))

    if want_state:
        @pl.when(ci == pl.num_programs(1) - 1)
        def _():
            co_ref[0] = c_ref[...]
            no_ref[0] = n_ref[...]
            mo_ref[0] = m_ref[...]


def _mlstm_scan(q, k, v, gates_t, init, *, want_state):
    b, l, _ = q.shape
    t = _ML_CHUNK
    nc = l // t
    fwd = lambda n: pl.BlockSpec((1, t, n), lambda bi, ci: (bi, ci, 0))
    bwd = lambda n: pl.BlockSpec((1, t, n), lambda bi, ci: (bi, nc - 1 - ci, 0))
    gtf = pl.BlockSpec((ML_GATES, t), lambda bi, ci: (0, bi * nc + ci))
    gtb = pl.BlockSpec((ML_GATES, t), lambda bi, ci: (0, bi * nc + nc - 1 - ci))
    in_specs = [fwd(ML_QK), fwd(ML_QK), fwd(ML_V), gtf, bwd(ML_QK), bwd(ML_QK), bwd(ML_V), gtb]
    args = [q, k, v, gates_t, q, k, v, gates_t]
    has_init = init is not None
    if has_init:
        c0, n0, m0 = init
        in_specs += [pl.BlockSpec((1, 8, ML_DK, ML_DV), lambda bi, ci: (bi, 0, 0, 0)),
                     pl.BlockSpec((1, 8, ML_DK), lambda bi, ci: (bi, 0, 0)),
                     pl.BlockSpec((1, 8, 1), lambda bi, ci: (bi, 0, 0))]
        args += [c0.reshape(b, 8, ML_DK, ML_DV), n0.reshape(b, 8, ML_DK), m0.reshape(b, 8, 1)]
    out_specs = [fwd(ML_V), bwd(ML_V)]
    out_shape = [jax.ShapeDtypeStruct((b, l, ML_V), _F32), jax.ShapeDtypeStruct((b, l, ML_V), _F32)]
    if want_state:
        out_specs += [pl.BlockSpec((1, 8, ML_DK, ML_DV), lambda bi, ci: (bi, 0, 0, 0)),
                      pl.BlockSpec((1, 8, ML_DK), lambda bi, ci: (bi, 0, 0)),
                      pl.BlockSpec((1, 8, 128), lambda bi, ci: (bi, 0, 0))]
        out_shape += [jax.ShapeDtypeStruct((b, 8, ML_DK, ML_DV), _F32),
                      jax.ShapeDtypeStruct((b, 8, ML_DK), _F32),
                      jax.ShapeDtypeStruct((b, 8, 128), _F32)]
    return pl.pallas_call(
        functools.partial(_mlstm_kernel, has_init=has_init, want_state=want_state),
        grid=(b, nc),
        in_specs=in_specs,
        out_specs=out_specs,
        out_shape=out_shape,
        scratch_shapes=[pltpu.VMEM((8, ML_DK, ML_DV), _F32), pltpu.VMEM((8, ML_DK), _F32),
                        pltpu.VMEM((8, 128), _F32)],
        compiler_params=_params(("arbitrary", "arbitrary")),
        name="mlstm_scan_state" if want_state else "mlstm_scan",
    )(*args)


def kernel(x_prompt, x_sample, c, cache_k, cache_v, state_C, state_n, state_m, c_ctx, ada_w, ada_b, ln_g, ln_b,
           da_w_qkv, da_lam, da_subln, da_w_o, ml_w_in, ml_conv_w, ml_conv_b, ml_b_gate, ml_norm_w, ml_w_out,
           ffn_w_up, ffn_conv_w, ffn_conv_b, ffn_w_down):
    bp, lp, _ = x_prompt.shape
    bs, ls, _ = x_sample.shape
    cond = jnp.concatenate([c_ctx[None, :], c, jnp.zeros((8 - 1 - bs, D_MODEL), _F32)], axis=0)
    mods = _modulation(cond, ada_w, ada_b).reshape(DEPTH, 8, 6, D_MODEL)

    xp = x_prompt.reshape(bp * lp, D_MODEL)
    xs = x_sample.reshape(bs * ls, D_MODEL)
    groups = (dict(seq_len=lp, seq_base=0), dict(seq_len=ls, seq_base=1))

    lam_init = 0.8 - 0.6 * math.exp(-0.3 * 0)
    w_qkv = da_w_qkv[0].astype(_BF)
    w_o = da_w_o[0].astype(_BF)
    qp, kp, vp = _qkv_proj(xp, mods[0], w_qkv, rope=False, kv_dtype=_F32, tm=_ROW_TILE, **groups[0])
    qs, ks, vs = _qkv_proj(xs, mods[0], w_qkv, rope=True, kv_dtype=_BF, tm=_ROW_TILE, **groups[1])
    as3 = lambda a, b: a.reshape(b, -1, D_MODEL)
    op = _diff_attention(as3(qp, bp), as3(kp, bp), as3(vp, bp), None, None, da_lam[0], da_subln[0],
                         lam_init=lam_init, tq=256, tk=512, heads=DA_HEADS)
    os_ = _diff_attention(as3(qs, bs), cache_k[:, 0].reshape(bs, -1, D_MODEL),
                          cache_v[:, 0].reshape(bs, -1, D_MODEL), as3(ks, bs), as3(vs, bs),
                          da_lam[0], da_subln[0], lam_init=lam_init, tq=256, tk=512, heads=1)
    xp = _outproj([op.reshape(-1, D_MODEL)], w_o, xp, mods[0], ln_g[0, 0], ln_b[0, 0], tm=_ROW_TILE, **groups[0])
    xs = _outproj([os_.reshape(-1, D_MODEL)], w_o, xs, mods[0], ln_g[0, 0], ln_b[0, 0], tm=_ROW_TILE, **groups[1])
    new_k = kp.reshape(bp, 1, lp, DA_HEADS, 2, DA_DK)
    new_v = vp.reshape(bp, 1, lp, DA_HEADS, DA_DV)

    def ffn(x2d, i, grp):
        return _conv_ffn(x2d, mods[i], ffn_w_up[i].astype(_BF), ffn_conv_w[i], ffn_conv_b[i],
                         ffn_w_down[i].astype(_BF), ln_g[i, 1], ln_b[i, 1], tm=min(_ROW_TILE, grp["seq_len"]), **grp)

    xp = ffn(xp, 0, groups[0])
    xs = ffn(xs, 0, groups[1])

    n_main = 2 * ML_QK + 2 * ML_V
    w_main = ml_w_in[0][:, :n_main].astype(_BF)
    w_gate = ml_w_in[0][:, n_main:]
    w_out = ml_w_out[0].astype(_BF)
    outs = []
    for x2d, grp, nb, init in ((xp, groups[0], bp, None),
                               (xs, groups[1], bs, (state_C[:, 0], state_n[:, 0], state_m[:, 0]))):
        q, k, v, og, gtt = _ml_proj(x2d, mods[1], w_main, w_gate, ml_conv_w[0], ml_conv_b[0], ml_b_gate[0],
                                        tm=min(_ROW_TILE, grp["seq_len"]), **grp)
        r3 = lambda a: a.reshape(nb, -1, a.shape[-1])
        res = _mlstm_scan(r3(q), r3(k), r3(v), gtt, init, want_state=init is None)
        hf, hb = res[0].reshape(-1, ML_V), res[1].reshape(-1, ML_V)
        x2d = _outproj([hf, hb, og], w_out, x2d, mods[1], ln_g[1, 0], ln_b[1, 0], tm=_ROW_TILE,
                       norm_w=ml_norm_w[0], **grp)
        outs.append((ffn(x2d, 1, grp), res[2:]))
    (xp, (c_fin, n_fin, m_fin)), (xs, _) = outs
    new_c = c_fin.reshape(bp, 1, 2, ML_HEADS, ML_DK, ML_DV)
    new_n = n_fin.reshape(bp, 1, 2, ML_HEADS, ML_DK)
    new_m = m_fin[:, :, 0].reshape(bp, 1, 2, ML_HEADS)
    return (xp.reshape(bp, lp, D_MODEL), xs.reshape(bs, ls, D_MODEL), new_k, new_v, new_c, new_n, new_m)
```

```python
import functools
import math

import jax
import jax.numpy as jnp
from jax import lax
from jax.experimental import pallas as pl
from jax.experimental.pallas import tpu as pltpu

D_MODEL = 1024
DEPTH = 2
GRID_W = 64
DA_HEADS = 8
DA_DK = 64
DA_DV = 128
ML_HEADS = 4
ML_DK = 128
ML_DV = 256
ML_QK = ML_HEADS * ML_DK
ML_V = ML_HEADS * ML_DV
ML_GATES = 16
D_FF = 2816
ROPE_THETA = 10000.0
ALPHA = (2 * DEPTH) ** 0.25
EPS = 1e-5

_BF = jnp.bfloat16
_F32 = jnp.float32
_NEG = -1e30

_VMEM_LIMIT_BYTES = 56 * 1024 * 1024
_HALO = 16
_ML_CHUNK = 256
_ROW_TILE = 512
_FF_CHUNK = 256
_Q_SCALE = DA_DK ** -0.5 * math.log2(math.e)


def _dot(a, b):
    return jnp.dot(a, b, preferred_element_type=_F32)


def _dot_nt(a, b):
    return lax.dot_general(a, b, (((1,), (1,)), ((), ())), preferred_element_type=_F32)


def _split2(x):
    hi = x.astype(_BF)
    lo = (x - hi.astype(_F32)).astype(_BF)
    return hi, lo


def _split3(x):
    hi = x.astype(_BF)
    r = x - hi.astype(_F32)
    mid = r.astype(_BF)
    lo = (r - mid.astype(_F32)).astype(_BF)
    return hi, mid, lo


def _dot_f32(a, b):
    ah, al = _split2(a)
    bh, bl = _split2(b)
    return _dot(ah, bh) + _dot(al, bh) + _dot(ah, bl)


def _dot_nt_f32(a, b):
    ah, al = _split2(a)
    bh, bl = _split2(b)
    return _dot_nt(ah, bh) + _dot_nt(al, bh) + _dot_nt(ah, bl)


def _sigmoid(x):
    return 1.0 / (1.0 + jnp.exp(-x))


def _silu(x):
    return x * _sigmoid(x)


def _log_sigmoid(x):
    return jnp.minimum(x, 0.0) - jnp.log(1.0 + jnp.exp(-jnp.abs(x)))


def _layer_norm_rows(z, g, b):
    mu = jnp.mean(z, axis=-1, keepdims=True)
    zc = z - mu
    var = jnp.mean(zc * zc, axis=-1, keepdims=True)
    return zc * lax.rsqrt(var + EPS) * g + b


def _params(sem):
    return pltpu.CompilerParams(dimension_semantics=sem, vmem_limit_bytes=_VMEM_LIMIT_BYTES)


def _mod_kernel(c_ref, w_ref, b_ref, o_ref):
    s = _silu(c_ref[...])
    o_ref[0] = _dot_f32(s, w_ref[0]) + b_ref[0]


def _modulation(cond, ada_w, ada_b):
    tn = 1024
    n = 6 * D_MODEL
    return pl.pallas_call(
        _mod_kernel,
        grid=(DEPTH, n // tn),
        in_specs=[
            pl.BlockSpec((8, D_MODEL), lambda l, j: (0, 0)),
            pl.BlockSpec((1, D_MODEL, tn), lambda l, j: (l, 0, j)),
            pl.BlockSpec((1, 1, tn), lambda l, j: (l, 0, j)),
        ],
        out_specs=pl.BlockSpec((1, 8, tn), lambda l, j: (l, 0, j)),
        out_shape=jax.ShapeDtypeStruct((DEPTH, 8, n), _F32),
        compiler_params=_params(("arbitrary", "arbitrary")),
        name="adaln_mod",
    )(cond, ada_w, ada_b.reshape(DEPTH, 1, n))


def _mod_spec(seq_base, tiles_per_seq):
    if tiles_per_seq is None:
        return pl.BlockSpec((1, 6, D_MODEL), lambda i, *_: (seq_base, 0, 0))
    return pl.BlockSpec((1, 6, D_MODEL), lambda i, *_: (seq_base + i // tiles_per_seq, 0, 0))


def _qkv_kernel(*refs, rope):
    if rope:
        x_ref, m_ref, w_ref, cos_ref, sa_ref, sb_ref, q_ref, k_ref, v_ref = refs
    else:
        x_ref, m_ref, w_ref, q_ref, k_ref, v_ref = refs
    h = (x_ref[...] * (1.0 + m_ref[0, 1:2, :]) + m_ref[0, 0:1, :]).astype(_BF)
    for c, o_ref in enumerate((q_ref, k_ref, v_ref)):
        y = _dot(h, w_ref[:, c * D_MODEL:(c + 1) * D_MODEL])
        if rope and c < 2:
            cos, sa, sb = cos_ref[...], sa_ref[...], sb_ref[...]
            for hd in range(DA_HEADS):
                yh = y[:, hd * 128:(hd + 1) * 128]
                yh = yh * cos + pltpu.roll(yh, 112, 1) * sa + pltpu.roll(yh, 16, 1) * sb
                if c == 0:
                    yh = yh * _Q_SCALE
                o_ref[:, hd * 128:(hd + 1) * 128] = yh.astype(o_ref.dtype)
        else:
            if c == 0:
                y = y * _Q_SCALE
            o_ref[...] = y.astype(o_ref.dtype)


def _rope_tables(seq_len):
    t = jnp.arange(seq_len)
    row = (t // GRID_W).astype(_F32)
    col = (t % GRID_W).astype(_F32)
    lane = jnp.arange(128)
    d = lane % DA_DK
    half = DA_DK // 2
    nf = half // 2
    dd = d % half
    f = dd % nf
    odd = (dd // nf) == 1
    inv = ROPE_THETA ** (-jnp.arange(nf, dtype=_F32) / nf)
    pos = jnp.where((d < half)[None, :], row[:, None], col[:, None])
    ang = pos * inv[f][None, :]
    cos, sin = jnp.cos(ang), jnp.sin(ang)
    sa = jnp.where(odd[None, :], 0.0, -sin)
    sb = jnp.where(odd[None, :], sin, 0.0)
    return cos, sa, sb


def _qkv_proj(x2d, mods, w_bf, *, seq_len, seq_base, rope, kv_dtype, tm):
    rows = x2d.shape[0]
    tps = seq_len // tm if seq_base else None
    in_specs = [
        pl.BlockSpec((tm, D_MODEL), lambda i: (i, 0)),
        _mod_spec(seq_base, tps),
        pl.BlockSpec((D_MODEL, 3 * D_MODEL), lambda i: (0, 0)),
    ]
    args = [x2d, mods, w_bf]
    if rope:
        tabs = _rope_tables(seq_len)
        in_specs += [pl.BlockSpec((tm, 128), lambda i: (i % (seq_len // tm), 0))] * 3
        args += list(tabs)
    out_spec = pl.BlockSpec((tm, D_MODEL), lambda i: (i, 0))
    return pl.pallas_call(
        functools.partial(_qkv_kernel, rope=rope),
        grid=(rows // tm,),
        in_specs=in_specs,
        out_specs=[out_spec, out_spec, out_spec],
        out_shape=[jax.ShapeDtypeStruct((rows, D_MODEL), _BF),
                   jax.ShapeDtypeStruct((rows, D_MODEL), kv_dtype),
                   jax.ShapeDtypeStruct((rows, D_MODEL), kv_dtype)],
        compiler_params=_params(("arbitrary",)),
        name="da_qkv_rope" if rope else "da_qkv",
    )(*args)


def _group_rows(x):
    return x.reshape(x.shape[0] // 8, 8, x.shape[1])


def _attn_kernel(*refs, n_new, tk, lam_init, heads):
    if n_new:
        q_ref, ka_ref, va_ref, kb_ref, vb_ref, lam_ref, sub_ref, o_ref, vta_ref, vtb_ref, s_ref = refs
    else:
        q_ref, ka_ref, va_ref, lam_ref, sub_ref, o_ref, vta_ref, s_ref = refs
    la = ka_ref.shape[1]
    lanes = lambda hd: slice(hd * 128, (hd + 1) * 128)

    @pl.when(pl.program_id(2) == 0)
    def _():
        for hd in range(heads):
            vta_ref[hd] = va_ref[0, :, lanes(hd)].astype(_F32).T.astype(_BF)
            for j in range(n_new):
                vtb_ref[hd, j] = vb_ref[0, j * tk:(j + 1) * tk, lanes(hd)].astype(_F32).T.astype(_BF)

    lf = lam_ref[...]
    lam = (jnp.exp(jnp.sum(lf[0:1] * lf[1:2], axis=1, keepdims=True))
           - jnp.exp(jnp.sum(lf[2:3] * lf[3:4], axis=1, keepdims=True)) + lam_init)

    for hd in range(heads):
        q = q_ref[0, :, lanes(hd)]
        first_map = lax.broadcasted_iota(jnp.int32, q.shape, 1) < DA_DK
        qm = (jnp.where(first_map, q, jnp.zeros_like(q)), jnp.where(first_map, jnp.zeros_like(q), q))

        tiles = [(0, la, lambda: ka_ref[0, :, lanes(hd)].astype(_BF), lambda: vta_ref[hd])]
        for j in range(n_new):
            tiles.append((la + j * tk, tk, lambda j=j: kb_ref[0, j * tk:(j + 1) * tk, lanes(hd)],
                          lambda j=j: vtb_ref[hd, j]))

        def scores(mp, tile, mrun):
            off, rows, load_k, _ = tile
            s = _dot_nt(load_k(), qm[mp])
            s_ref[hd, mp, off:off + rows, :] = s
            m = jnp.max(_group_rows(s), axis=0)
            return m if mrun is None else jnp.maximum(mrun, m)

        def values(mp, tile, mx, state):
            off, rows, _, load_vt = tile
            e = jnp.exp2(s_ref[hd, mp, off:off + rows, :] - mx)
            pv = _dot(load_vt(), e.astype(_BF))
            ls = jnp.sum(_group_rows(e), axis=0)
            return (pv, ls) if state is None else (state[0] + pv, state[1] + ls)

        m0 = m1 = st0 = st1 = None
        for tl in tiles:
            m0 = scores(0, tl, m0)
            m1 = scores(1, tl, m1)
        mx0 = jnp.max(m0, axis=0, keepdims=True)
        mx1 = jnp.max(m1, axis=0, keepdims=True)
        for tl in tiles:
            st0 = values(0, tl, mx0, st0)
            st1 = values(1, tl, mx1, st1)
        l0 = jnp.sum(st0[1], axis=0, keepdims=True)
        l1 = jnp.sum(st1[1], axis=0, keepdims=True)

        o = st0[0] * (1.0 / l0) - st1[0] * (lam / l1)
        ms = jnp.mean(o * o, axis=0, keepdims=True)
        o = o * lax.rsqrt(ms + EPS) * (sub_ref[...] * (1.0 - lam_init))
        o_ref[0, :, lanes(hd)] = o.T.astype(o_ref.dtype)


def _diff_attention(q, ka, va, kb, vb, lam, subln, *, lam_init, tq, tk, heads):
    b, lq, _ = q.shape
    la = ka.shape[1]
    n_new = 0 if kb is None else kb.shape[1] // tk
    width = 128 * heads
    hspec = lambda rows: pl.BlockSpec((1, rows, width), lambda bi, hi, qi: (bi, 0, hi))
    in_specs = [pl.BlockSpec((1, tq, width), lambda bi, hi, qi: (bi, qi, hi)), hspec(la), hspec(la)]
    args = [q, ka, va]
    scratch = [pltpu.VMEM((heads, 128, la), _BF)]
    if n_new:
        in_specs += [hspec(kb.shape[1]), hspec(kb.shape[1])]
        args += [kb, vb]
        scratch.append(pltpu.VMEM((heads, n_new, 128, tk), _BF))
    in_specs += [pl.BlockSpec((4, DA_DK), lambda bi, hi, qi: (0, 0)),
                 pl.BlockSpec((DA_DV, 1), lambda bi, hi, qi: (0, 0))]
    args += [lam, subln.reshape(DA_DV, 1)]
    lk = la + (kb.shape[1] if n_new else 0)
    scratch.append(pltpu.VMEM((heads, 2, lk, tq), _F32))
    return pl.pallas_call(
        functools.partial(_attn_kernel, n_new=n_new, tk=tk, lam_init=lam_init, heads=heads),
        grid=(b, DA_HEADS // heads, lq // tq),
        in_specs=in_specs,
        out_specs=pl.BlockSpec((1, tq, width), lambda bi, hi, qi: (bi, qi, hi)),
        out_shape=jax.ShapeDtypeStruct((b, lq, D_MODEL), _BF),
        scratch_shapes=scratch,
        compiler_params=_params(("arbitrary", "arbitrary", "arbitrary")),
        name="diff_attn_ctx" if n_new else "diff_attn",
    )(*args)


def _exact_zero_like(x):
    u = lax.bitcast_convert_type(x, jnp.uint32)
    u = lax.shift_right_logical(lax.shift_right_logical(u, jnp.uint32(16)), jnp.uint32(16))
    return lax.bitcast_convert_type(u, _F32)


def _attn_ctx_kernel(q_ref, ka_ref, kb_ref, va_ref, vb_ref, lam_ref, sub_ref, o_ref,
                     vta_ref, vtb_ref, s0_ref, s1_ref, mx0_ref, mx1_ref, *, nq, tk, lam_init):
    t = pl.program_id(0)
    la = ka_ref.shape[1]
    n_new = kb_ref.shape[1] // tk

    @pl.when(t == 0)
    def _():
        s1_ref[...] = jnp.zeros(s1_ref.shape, _F32)
        mx1_ref[...] = jnp.zeros(mx1_ref.shape, _F32)

    @pl.when(jnp.maximum(t - 1, 0) % nq == 0)
    def _():
        vta_ref[...] = va_ref[0].astype(_F32).T.astype(_BF)
        for j in range(n_new):
            vtb_ref[j] = vb_ref[0, j * tk:(j + 1) * tk, :].astype(_F32).T.astype(_BF)

    lf = lam_ref[...]
    lam = (jnp.exp(jnp.sum(lf[0:1] * lf[1:2], axis=1, keepdims=True))
           - jnp.exp(jnp.sum(lf[2:3] * lf[3:4], axis=1, keepdims=True)) + lam_init)

    tiles = [(0, la, lambda: ka_ref[0].astype(_BF), lambda: vta_ref[...])]
    for j in range(n_new):
        tiles.append((la + j * tk, tk, lambda j=j: kb_ref[0, j * tk:(j + 1) * tk, :], lambda j=j: vtb_ref[j]))

    def step(s_w, mx_w, s_r, mx_r):
        q = q_ref[0]
        first_map = lax.broadcasted_iota(jnp.int32, q.shape, 1) < DA_DK
        qm = (jnp.where(first_map, q, jnp.zeros_like(q)), jnp.where(first_map, jnp.zeros_like(q), q))
        mx = [jnp.max(mx_r[mp], axis=0, keepdims=True) for mp in range(2)]
        mrun, acc, lsum = [None, None], [None, None], [None, None]
        pace = None
        for off, rows, load_k, load_vt in tiles:
            kt = load_k()
            if pace is not None:
                kt = kt + pace
            for mp in range(2):
                s = _dot_nt(kt, qm[mp])
                s_w[mp, off:off + rows, :] = s
                m = jnp.max(_group_rows(s), axis=0)
                mrun[mp] = m if mrun[mp] is None else jnp.maximum(mrun[mp], m)
            for mp in range(2):
                e = jnp.exp2(s_r[mp, off:off + rows, :] - mx[mp])
                pv = _dot(load_vt(), e.astype(_BF))
                ls = jnp.sum(_group_rows(e), axis=0)
                acc[mp] = pv if acc[mp] is None else acc[mp] + pv
                lsum[mp] = ls if lsum[mp] is None else lsum[mp] + ls
            pace = _exact_zero_like(ls[0:1, 0:128]).astype(_BF)
        for mp in range(2):
            mx_w[mp] = mrun[mp]
        l0 = jnp.sum(lsum[0], axis=0, keepdims=True)
        l1 = jnp.sum(lsum[1], axis=0, keepdims=True)
        o = acc[0] * (1.0 / l0) - acc[1] * (lam / l1)
        ms = jnp.mean(o * o, axis=0, keepdims=True)
        o = o * lax.rsqrt(ms + EPS) * (sub_ref[...] * (1.0 - lam_init))
        o_ref[0] = o.T.astype(o_ref.dtype)

    @pl.when(t % 2 == 0)
    def _():
        step(s0_ref, mx0_ref, s1_ref, mx1_ref)

    @pl.when(t % 2 == 1)
    def _():
        step(s1_ref, mx1_ref, s0_ref, mx0_ref)


def _diff_attention_ctx(q, ka, va, kb, vb, lam, subln, *, lam_init, tq, tk):
    b, lq, _ = q.shape
    la, lb = ka.shape[1], kb.shape[1]
    nq = lq // tq
    units = b * DA_HEADS * nq

    def unit_index(u):
        bh = u // nq
        return bh // DA_HEADS, bh % DA_HEADS, u % nq

    def score_unit(t):
        return unit_index(jnp.minimum(t, units - 1))

    def value_unit(t):
        return unit_index(jnp.maximum(t - 1, 0))

    def q_map(t):
        bi, hi, qi = score_unit(t)
        return bi, qi, hi

    def k_map(t):
        bi, hi, _ = score_unit(t)
        return bi, 0, hi

    def v_map(t):
        bi, hi, _ = value_unit(t)
        return bi, 0, hi

    def o_map(t):
        bi, hi, qi = value_unit(t)
        return bi, qi, hi

    score_buf = pltpu.VMEM((2, la + lb, tq), _F32)
    max_buf = pltpu.VMEM((2, 8, tq), _F32)
    return pl.pallas_call(
        functools.partial(_attn_ctx_kernel, nq=nq, tk=tk, lam_init=lam_init),
        grid=(units + 1,),
        in_specs=[pl.BlockSpec((1, tq, 128), q_map),
                  pl.BlockSpec((1, la, 128), k_map), pl.BlockSpec((1, lb, 128), k_map),
                  pl.BlockSpec((1, la, 128), v_map), pl.BlockSpec((1, lb, 128), v_map),
                  pl.BlockSpec((4, DA_DK), lambda t: (0, 0)), pl.BlockSpec((DA_DV, 1), lambda t: (0, 0))],
        out_specs=pl.BlockSpec((1, tq, 128), o_map),
        out_shape=jax.ShapeDtypeStruct((b, lq, D_MODEL), _BF),
        scratch_shapes=[pltpu.VMEM((128, la), _BF), pltpu.VMEM((lb // tk, 128, tk), _BF),
                        score_buf, score_buf, max_buf, max_buf],
        compiler_params=_params(("arbitrary",)),
        name="diff_attn_ctx",
    )(q, ka, kb, va, vb, lam, subln.reshape(DA_DV, 1))


def _outproj_kernel(*refs, mlstm):
    if mlstm:
        hf_ref, hb_ref, og_ref, nw_ref, w_ref, x_ref, m_ref, g_ref, b_ref, o_ref = refs
        h = hf_ref[...] + hb_ref[...]
        parts = []
        for hd in range(ML_HEADS):
            hh = h[:, hd * ML_DV:(hd + 1) * ML_DV]
            mu = jnp.mean(hh, axis=-1, keepdims=True)
            hc = hh - mu
            var = jnp.mean(hc * hc, axis=-1, keepdims=True)
            parts.append(hc * lax.rsqrt(var + EPS))
        hn = jnp.concatenate(parts, axis=-1) * nw_ref[...]
        a = (og_ref[...].astype(_F32) * hn).astype(_BF)
    else:
        a_ref, w_ref, x_ref, m_ref, g_ref, b_ref, o_ref = refs
        a = a_ref[...]
    y = _dot(a, w_ref[...])
    z = ALPHA * x_ref[...] + m_ref[0, 2:3, :] * y
    o_ref[...] = _layer_norm_rows(z, g_ref[...], b_ref[...])


def _outproj(acts, w_bf, x2d, mods, ln_g, ln_b, *, seq_len, seq_base, tm, norm_w=None):
    rows = x2d.shape[0]
    mlstm = norm_w is not None
    tps = seq_len // tm if seq_base else None
    row_spec = pl.BlockSpec((tm, D_MODEL), lambda i: (i, 0))
    vec_spec = pl.BlockSpec((1, D_MODEL), lambda i: (0, 0))
    in_specs = [row_spec] * len(acts)
    args = list(acts)
    if mlstm:
        in_specs.append(vec_spec)
        args.append(norm_w.reshape(1, D_MODEL))
    in_specs += [pl.BlockSpec((D_MODEL, D_MODEL), lambda i: (0, 0)), row_spec, _mod_spec(seq_base, tps),
                 vec_spec, vec_spec]
    args += [w_bf, x2d, mods, ln_g.reshape(1, D_MODEL), ln_b.reshape(1, D_MODEL)]
    return pl.pallas_call(
        functools.partial(_outproj_kernel, mlstm=mlstm),
        grid=(rows // tm,),
        in_specs=in_specs,
        out_specs=row_spec,
        out_shape=jax.ShapeDtypeStruct((rows, D_MODEL), _F32),
        compiler_params=_params(("arbitrary",)),
        name="ml_outproj_ln" if mlstm else "da_outproj_ln",
    )(*args)


def _halo_specs(rows, tm):
    assert tm % _HALO == 0 and rows % tm == 0
    per = tm // _HALO
    nblk = rows // _HALO
    main = pl.BlockSpec((tm, D_MODEL), lambda i, *_: (i, 0))
    prev = pl.BlockSpec((_HALO, D_MODEL), lambda i, *_: (jnp.maximum(i * per - 1, 0), 0))
    nxt = pl.BlockSpec((_HALO, D_MODEL), lambda i, *_: (jnp.minimum((i + 1) * per, nblk - 1), 0))
    return prev, main, nxt


def _modulated_ext(prev_ref, main_ref, next_ref, shift, scale, seq_len):
    tm = main_ref.shape[0]
    tiles = seq_len // tm
    t = pl.program_id(0) % tiles
    keep_prev = jnp.where(t != 0, 1.0, 0.0)
    keep_next = jnp.where(t != tiles - 1, 1.0, 0.0)
    mod = lambda r: r * (1.0 + scale) + shift
    ext = jnp.concatenate([mod(prev_ref[...]) * keep_prev, mod(main_ref[...]), mod(next_ref[...]) * keep_next],
                          axis=0)
    return ext.astype(_BF)


def _conv3(u, w, b, tm):
    rows = u.shape[0]
    up = pltpu.roll(u, 1, 0)[_HALO:_HALO + tm]
    un = pltpu.roll(u, rows - 1, 0)[_HALO:_HALO + tm]
    return up * w[0:1, :] + u[_HALO:_HALO + tm] * w[1:2, :] + un * w[2:3, :] + b


def _resident(shape):
    return pl.BlockSpec(shape, lambda i: (0,) * len(shape), pipeline_mode=pl.Buffered(1))


def _ffn_kernel(prev_ref, x_ref, next_ref, m_ref, wu_ref, cw_ref, cb_ref, wd_ref, g_ref, b_ref, o_ref,
                a_ref, *, seq_len):
    tm = x_ref.shape[0]
    h = _modulated_ext(prev_ref, x_ref, next_ref, m_ref[0, 3:4, :], m_ref[0, 4:5, :], seq_len)
    for j in range(D_FF // _FF_CHUNK):
        g0, v0 = j * _FF_CHUNK, D_FF + j * _FF_CHUNK
        gate = _conv3(_dot(h, wu_ref[:, g0:g0 + _FF_CHUNK]), cw_ref[:, g0:g0 + _FF_CHUNK],
                      cb_ref[:, g0:g0 + _FF_CHUNK], tm)
        val = _conv3(_dot(h, wu_ref[:, v0:v0 + _FF_CHUNK]), cw_ref[:, v0:v0 + _FF_CHUNK],
                     cb_ref[:, v0:v0 + _FF_CHUNK], tm)
        a_ref[:, g0:g0 + _FF_CHUNK] = (_silu(gate) * val).astype(_BF)
    z = ALPHA * x_ref[...] + m_ref[0, 5:6, :] * _dot(a_ref[...], wd_ref[...])
    o_ref[...] = _layer_norm_rows(z, g_ref[...], b_ref[...])


def _conv_ffn(x2d, mods, w_up_bf, conv_w, conv_b, w_down_bf, ln_g, ln_b, *, seq_len, seq_base, tm):
    rows = x2d.shape[0]
    tps = seq_len // tm if seq_base else None
    prev, main, nxt = _halo_specs(rows, tm)
    vec_spec = pl.BlockSpec((1, D_MODEL), lambda i: (0, 0))
    return pl.pallas_call(
        functools.partial(_ffn_kernel, seq_len=seq_len),
        grid=(rows // tm,),
        in_specs=[
            prev, main, nxt, _mod_spec(seq_base, tps),
            _resident((D_MODEL, 2 * D_FF)), _resident((3, 2 * D_FF)), _resident((1, 2 * D_FF)),
            _resident((D_FF, D_MODEL)), vec_spec, vec_spec,
        ],
        out_specs=pl.BlockSpec((tm, D_MODEL), lambda i: (i, 0)),
        out_shape=jax.ShapeDtypeStruct((rows, D_MODEL), _F32),
        scratch_shapes=[pltpu.VMEM((tm, D_FF), _BF)],
        compiler_params=_params(("arbitrary",)),
        name="conv_ffn_ln",
    )(x2d, x2d, x2d, mods, w_up_bf, conv_w, conv_b.reshape(1, 2 * D_FF), w_down_bf,
      ln_g.reshape(1, D_MODEL), ln_b.reshape(1, D_MODEL))


def _mlproj_kernel(prev_ref, x_ref, next_ref, m_ref, w_ref, wgt_ref, cw_ref, cb_ref, bgt_ref,
                   q_ref, k_ref, v_ref, og_ref, gtt_ref, *, seq_len):
    tm = x_ref.shape[0]
    shift, scale = m_ref[0, 0:1, :], m_ref[0, 1:2, :]
    h = _modulated_ext(prev_ref, x_ref, next_ref, shift, scale, seq_len)
    qk = _silu(_conv3(_dot(h, w_ref[:, 0:2 * ML_QK]), cw_ref[...], cb_ref[...], tm))
    q_ref[...] = qk[:, 0:ML_QK].astype(_BF)
    k_ref[...] = (qk[:, ML_QK:2 * ML_QK] * (ML_DK ** -0.5)).astype(_BF)
    hm = h[_HALO:_HALO + tm, :]
    v_ref[...] = _dot(hm, w_ref[:, 2 * ML_QK:2 * ML_QK + ML_V]).astype(_BF)
    og_ref[...] = _sigmoid(_dot(hm, w_ref[:, 2 * ML_QK + ML_V:2 * ML_QK + 2 * ML_V])).astype(_BF)
    hf = x_ref[...] * (1.0 + scale) + shift
    gtt_ref[...] = _dot_nt_f32(wgt_ref[...], hf) + bgt_ref[...]


def _ml_proj(x2d, mods, w_main_bf, w_gate, conv_w, conv_b, b_gate, *, seq_len, seq_base, tm):
    rows = x2d.shape[0]
    tps = seq_len // tm if seq_base else None
    prev, main, nxt = _halo_specs(rows, tm)
    n_main = 2 * ML_QK + 2 * ML_V
    full = lambda shape: pl.BlockSpec(shape, lambda i: (0,) * len(shape))
    row = lambda n: pl.BlockSpec((tm, n), lambda i: (i, 0))
    return pl.pallas_call(
        functools.partial(_mlproj_kernel, seq_len=seq_len),
        grid=(rows // tm,),
        in_specs=[prev, main, nxt, _mod_spec(seq_base, tps), full((D_MODEL, n_main)),
                  full((ML_GATES, D_MODEL)), full((3, 2 * ML_QK)), full((1, 2 * ML_QK)), full((ML_GATES, 1))],
        out_specs=[row(ML_QK), row(ML_QK), row(ML_V), row(ML_V),
                   pl.BlockSpec((ML_GATES, tm), lambda i: (0, i))],
        out_shape=[jax.ShapeDtypeStruct((rows, ML_QK), _BF), jax.ShapeDtypeStruct((rows, ML_QK), _BF),
                   jax.ShapeDtypeStruct((rows, ML_V), _BF), jax.ShapeDtypeStruct((rows, ML_V), _BF),
                   jax.ShapeDtypeStruct((ML_GATES, rows), _F32)],
        compiler_params=_params(("arbitrary",)),
        name="ml_inproj_conv",
    )(x2d, x2d, x2d, mods, w_main_bf, w_gate.T, conv_w, conv_b.reshape(1, 2 * ML_QK),
      b_gate.reshape(ML_GATES, 1))


def _mlstm_chunk(q, k, v, i_row, b_col, b_row, total, m_prev, c_prev, n_prev, causal):
    t = q.shape[0]
    wide = lambda col: jnp.broadcast_to(col, (t, 128))
    tile2 = lambda w: jnp.concatenate([w] * (t // 128), axis=1)
    ti = lax.broadcasted_iota(jnp.int32, (t, t), 0)
    si = lax.broadcasted_iota(jnp.int32, (t, t), 1)
    a_row = i_row - b_row
    amat = jnp.where((si <= ti) if causal else (si >= ti), a_row, _NEG)
    g_col = jnp.maximum(m_prev, jnp.max(amat, axis=1, keepdims=True))
    g_w = wide(g_col)
    w_intra = jnp.exp(amat - tile2(g_w))
    w_inter_w = jnp.exp(m_prev - g_w)
    s = _dot_nt(q, k) * w_intra
    den_parts = w_inter_w * (q.astype(_F32) * n_prev)
    for c in range(t // 128):
        den_parts = den_parts + s[:, c * 128:(c + 1) * 128]
    den_col = jnp.sum(den_parts, axis=1, keepdims=True)
    r_col = 1.0 / jnp.maximum(jnp.abs(den_col), jnp.exp(-(b_col + g_col)))
    num = jnp.concatenate([w_inter_w, w_inter_w], axis=1) * _dot(q, c_prev.astype(_BF)) + _dot(s.astype(_BF), v)
    r_w = wide(r_col)
    h = num * jnp.concatenate([r_w, r_w], axis=1)
    g_end = jnp.maximum(m_prev, jnp.max(a_row, axis=1, keepdims=True))
    w_s = jnp.exp(a_row - g_end)
    carry = jnp.exp(m_prev - g_end)
    kw_t = (k.astype(_F32).T * w_s).astype(_BF)
    c_new = carry * c_prev + _dot(kw_t, v)
    ws_hi, ws_lo = _split2(jnp.broadcast_to(w_s, (8, t)))
    n_new = carry * n_prev + (_dot(ws_hi, k) + _dot(ws_lo, k))[0:1, :]
    return h, c_new, n_new, total + g_end


def _mlstm_kernel(*refs, has_init, want_state):
    refs = list(refs)
    (qf_ref, kf_ref, vf_ref, gtf_ref, qb_ref, kb_ref, vb_ref, gtb_ref) = refs[:8]
    pos = 8
    if has_init:
        c0_ref, n0_ref, m0_ref = refs[pos:pos + 3]
        pos += 3
    hf_ref, hb_ref = refs[pos:pos + 2]
    pos += 2
    if want_state:
        co_ref, no_ref, mo_ref = refs[pos:pos + 3]
        pos += 3
    c_ref, n_ref, m_ref = refs[pos:pos + 3]
    ci = pl.program_id(1)
    t = qf_ref.shape[1]

    @pl.when(ci == 0)
    def _():
        if has_init:
            c_ref[...] = c0_ref[0]
            n_ref[...] = n0_ref[0]
            m_ref[...] = jnp.broadcast_to(m0_ref[0], m_ref.shape)
        else:
            c_ref[...] = jnp.zeros(c_ref.shape, _F32)
            n_ref[...] = jnp.zeros(n_ref.shape, _F32)
            m_ref[...] = jnp.zeros(m_ref.shape, _F32)

    ri = lax.broadcasted_iota(jnp.int32, (t, t), 0)
    cj = lax.broadcasted_iota(jnp.int32, (t, t), 1)
    lower = (cj <= ri).astype(_BF)
    upper = (cj >= ri).astype(_BF)

    def cum(mat, xt):
        return sum(_dot_nt(mat, p) for p in _split3(xt))

    def cum_t(xt, mat):
        return sum(_dot(p, mat) for p in _split3(xt))

    for d, (q_ref, k_ref, v_ref, gt_ref, h_ref) in enumerate(
            ((qf_ref, kf_ref, vf_ref, gtf_ref, hf_ref), (qb_ref, kb_ref, vb_ref, gtb_ref, hb_ref))):
        gt = gt_ref[...]
        is_f_row = (lax.broadcasted_iota(jnp.int32, (ML_GATES, 1), 0) % 8) >= 4
        xt = jnp.where(is_f_row, _log_sigmoid(gt), gt)
        if d == 0:
            bc = cum(lower, xt)
            br = cum_t(xt, upper)
        else:
            bc = cum(upper, xt)
            br = cum_t(xt, lower)
        tot = jnp.sum(xt, axis=1, keepdims=True)
        for hd in range(ML_HEADS):
            ic, fc = d * 8 + hd, d * 8 + 4 + hd
            st = d * ML_HEADS + hd
            h, c_new, n_new, m_new = _mlstm_chunk(
                q_ref[0, :, hd * ML_DK:(hd + 1) * ML_DK], k_ref[0, :, hd * ML_DK:(hd + 1) * ML_DK],
                v_ref[0, :, hd * ML_DV:(hd + 1) * ML_DV],
                xt[ic:ic + 1, :], bc[:, fc:fc + 1], br[fc:fc + 1, :],
                tot[fc:fc + 1, :], m_ref[st:st + 1, 0:1], c_ref[st], n_ref[st:st + 1, :], d == 0)
            h_ref[0, :, hd * ML_DV:(hd + 1) * ML_DV] = h
            c_ref[st] = c_new
            n_ref[st:st + 1, :] = n_new
            m_ref[st:st + 1, :] = jnp.broadcast_to(m_new, (1, 128))

    if want_state:
        @pl.when(ci == pl.num_programs(1) - 1)
        def _():
            co_ref[0] = c_ref[...]
            no_ref[0] = n_ref[...]
            mo_ref[0] = m_ref[...]


def _mlstm_scan(q, k, v, gates_t, init, *, want_state):
    b, l, _ = q.shape
    t = _ML_CHUNK
    nc = l // t
    fwd = lambda n: pl.BlockSpec((1, t, n), lambda bi, ci: (bi, ci, 0))
    bwd = lambda n: pl.BlockSpec((1, t, n), lambda bi, ci: (bi, nc - 1 - ci, 0))
    gtf = pl.BlockSpec((ML_GATES, t), lambda bi, ci: (0, bi * nc + ci))
    gtb = pl.BlockSpec((ML_GATES, t), lambda bi, ci: (0, bi * nc + nc - 1 - ci))
    in_specs = [fwd(ML_QK), fwd(ML_QK), fwd(ML_V), gtf, bwd(ML_QK), bwd(ML_QK), bwd(ML_V), gtb]
    args = [q, k, v, gates_t, q, k, v, gates_t]
    has_init = init is not None
    if has_init:
        c0, n0, m0 = init
        in_specs += [pl.BlockSpec((1, 8, ML_DK, ML_DV), lambda bi, ci: (bi, 0, 0, 0)),
                     pl.BlockSpec((1, 8, ML_DK), lambda bi, ci: (bi, 0, 0)),
                     pl.BlockSpec((1, 8, 1), lambda bi, ci: (bi, 0, 0))]
        args += [c0.reshape(b, 8, ML_DK, ML_DV), n0.reshape(b, 8, ML_DK), m0.reshape(b, 8, 1)]
    out_specs = [fwd(ML_V), bwd(ML_V)]
    out_shape = [jax.ShapeDtypeStruct((b, l, ML_V), _F32), jax.ShapeDtypeStruct((b, l, ML_V), _F32)]
    if want_state:
        out_specs += [pl.BlockSpec((1, 8, ML_DK, ML_DV), lambda bi, ci: (bi, 0, 0, 0)),
                      pl.BlockSpec((1, 8, ML_DK), lambda bi, ci: (bi, 0, 0)),
                      pl.BlockSpec((1, 8, 128), lambda bi, ci: (bi, 0, 0))]
        out_shape += [jax.ShapeDtypeStruct((b, 8, ML_DK, ML_DV), _F32),
                      jax.ShapeDtypeStruct((b, 8, ML_DK), _F32),
                      jax.ShapeDtypeStruct((b, 8, 128), _F32)]
    return pl.pallas_call(
        functools.partial(_mlstm_kernel, has_init=has_init, want_state=want_state),
        grid=(b, nc),
        in_specs=in_specs,
        out_specs=out_specs,
        out_shape=out_shape,
        scratch_shapes=[pltpu.VMEM((8, ML_DK, ML_DV), _F32), pltpu.VMEM((8, ML_DK), _F32),
                        pltpu.VMEM((8, 128), _F32)],
        compiler_params=_params(("arbitrary", "arbitrary")),
        name="mlstm_scan_state" if want_state else "mlstm_scan",
    )(*args)


def kernel(x_prompt, x_sample, c, cache_k, cache_v, state_C, state_n, state_m, c_ctx, ada_w, ada_b, ln_g, ln_b,
           da_w_qkv, da_lam, da_subln, da_w_o, ml_w_in, ml_conv_w, ml_conv_b, ml_b_gate, ml_norm_w, ml_w_out,
           ffn_w_up, ffn_conv_w, ffn_conv_b, ffn_w_down):
    bp, lp, _ = x_prompt.shape
    bs, ls, _ = x_sample.shape
    cond = jnp.concatenate([c_ctx[None, :], c, jnp.zeros((8 - 1 - bs, D_MODEL), _F32)], axis=0)
    mods = _modulation(cond, ada_w, ada_b).reshape(DEPTH, 8, 6, D_MODEL)

    xp = x_prompt.reshape(bp * lp, D_MODEL)
    xs = x_sample.reshape(bs * ls, D_MODEL)
    groups = (dict(seq_len=lp, seq_base=0), dict(seq_len=ls, seq_base=1))

    lam_init = 0.8 - 0.6 * math.exp(-0.3 * 0)
    w_qkv = da_w_qkv[0].astype(_BF)
    w_o = da_w_o[0].astype(_BF)
    qp, kp, vp = _qkv_proj(xp, mods[0], w_qkv, rope=False, kv_dtype=_F32, tm=_ROW_TILE, **groups[0])
    qs, ks, vs = _qkv_proj(xs, mods[0], w_qkv, rope=True, kv_dtype=_BF, tm=_ROW_TILE, **groups[1])
    as3 = lambda a, b: a.reshape(b, -1, D_MODEL)
    op = _diff_attention(as3(qp, bp), as3(kp, bp), as3(vp, bp), None, None, da_lam[0], da_subln[0],
                         lam_init=lam_init, tq=256, tk=512, heads=DA_HEADS)
    os_ = _diff_attention_ctx(as3(qs, bs), cache_k[:, 0].reshape(bs, -1, D_MODEL),
                              cache_v[:, 0].reshape(bs, -1, D_MODEL), as3(ks, bs), as3(vs, bs),
                              da_lam[0], da_subln[0], lam_init=lam_init, tq=256, tk=512)
    xp = _outproj([op.reshape(-1, D_MODEL)], w_o, xp, mods[0], ln_g[0, 0], ln_b[0, 0], tm=_ROW_TILE, **groups[0])
    xs = _outproj([os_.reshape(-1, D_MODEL)], w_o, xs, mods[0], ln_g[0, 0], ln_b[0, 0], tm=_ROW_TILE, **groups[1])
    new_k = kp.reshape(bp, 1, lp, DA_HEADS, 2, DA_DK)
    new_v = vp.reshape(bp, 1, lp, DA_HEADS, DA_DV)

    def ffn(x2d, i, grp):
        return _conv_ffn(x2d, mods[i], ffn_w_up[i].astype(_BF), ffn_conv_w[i], ffn_conv_b[i],
                         ffn_w_down[i].astype(_BF), ln_g[i, 1], ln_b[i, 1], tm=min(_ROW_TILE, grp["seq_len"]), **grp)

    xp = ffn(xp, 0, groups[0])
    xs = ffn(xs, 0, groups[1])

    n_main = 2 * ML_QK + 2 * ML_V
    w_main = ml_w_in[0][:, :n_main].astype(_BF)
    w_gate = ml_w_in[0][:, n_main:]
    w_out = ml_w_out[0].astype(_BF)
    outs = []
    for x2d, grp, nb, init in ((xp, groups[0], bp, None),
                               (xs, groups[1], bs, (state_C[:, 0], state_n[:, 0], state_m[:, 0]))):
        q, k, v, og, gtt = _ml_proj(x2d, mods[1], w_main, w_gate, ml_conv_w[0], ml_conv_b[0], ml_b_gate[0],
                                        tm=min(_ROW_TILE, grp["seq_len"]), **grp)
        r3 = lambda a: a.reshape(nb, -1, a.shape[-1])
        res = _mlstm_scan(r3(q), r3(k), r3(v), gtt, init, want_state=init is None)
        hf, hb = res[0].reshape(-1, ML_V), res[1].reshape(-1, ML_V)
        x2d = _outproj([hf, hb, og], w_out, x2d, mods[1], ln_g[1, 0], ln_b[1, 0], tm=_ROW_TILE,
                       norm_w=ml_norm_w[0], **grp)
        outs.append((ffn(x2d, 1, grp), res[2:]))
    (xp, (c_fin, n_fin, m_fin)), (xs, _) = outs
    new_c = c_fin.reshape(bp, 1, 2, ML_HEADS, ML_DK, ML_DV)
    new_n = n_fin.reshape(bp, 1, 2, ML_HEADS, ML_DK)
    new_m = m_fin[:, :, 0].reshape(bp, 1, 2, ML_HEADS)
    return (xp.reshape(bp, lp, D_MODEL), xs.reshape(bs, ls, D_MODEL), new_k, new_v, new_c, new_n, new_m)
```

```python
import functools
import math

import jax
import jax.numpy as jnp
from jax import lax
from jax.experimental import pallas as pl
from jax.experimental.pallas import tpu as pltpu

D_MODEL = 1024
DEPTH = 2
GRID_W = 64
DA_HEADS = 8
DA_DK = 64
DA_DV = 128
ML_HEADS = 4
ML_DK = 128
ML_DV = 256
ML_QK = ML_HEADS * ML_DK
ML_V = ML_HEADS * ML_DV
ML_GATES = 16
D_FF = 2816
ROPE_THETA = 10000.0
ALPHA = (2 * DEPTH) ** 0.25
EPS = 1e-5

_BF = jnp.bfloat16
_F32 = jnp.float32
_NEG = -1e30

_VMEM_LIMIT_BYTES = 56 * 1024 * 1024
_HALO = 8
_MIX_HALO = 16
_ML_CHUNK = 256
_ROW_TILE = 512
_FF_CHUNK = 256
_Q_SCALE = DA_DK ** -0.5 * math.log2(math.e)


def _dot(a, b):
    return jnp.dot(a, b, preferred_element_type=_F32)


def _dot_nt(a, b):
    return lax.dot_general(a, b, (((1,), (1,)), ((), ())), preferred_element_type=_F32)


def _split2(x):
    hi = x.astype(_BF)
    lo = (x - hi.astype(_F32)).astype(_BF)
    return hi, lo


def _split3(x):
    hi = x.astype(_BF)
    r = x - hi.astype(_F32)
    mid = r.astype(_BF)
    lo = (r - mid.astype(_F32)).astype(_BF)
    return hi, mid, lo


def _dot_f32(a, b):
    ah, al = _split2(a)
    bh, bl = _split2(b)
    return _dot(ah, bh) + _dot(al, bh) + _dot(ah, bl)


def _sigmoid(x):
    return 1.0 / (1.0 + jnp.exp(-x))


def _silu(x):
    return x * _sigmoid(x)


def _log_sigmoid(x):
    return jnp.minimum(x, 0.0) - jnp.log(1.0 + jnp.exp(-jnp.abs(x)))


def _layer_norm_rows(z, g, b):
    mu = jnp.mean(z, axis=-1, keepdims=True)
    zc = z - mu
    var = jnp.mean(zc * zc, axis=-1, keepdims=True)
    return zc * lax.rsqrt(var + EPS) * g + b


def _params(sem):
    return pltpu.CompilerParams(dimension_semantics=sem, vmem_limit_bytes=_VMEM_LIMIT_BYTES)


def _mod_kernel(c_ref, w_ref, b_ref, o_ref):
    s = _silu(c_ref[...])
    o_ref[0] = _dot_f32(s, w_ref[0]) + b_ref[0]


def _modulation(cond, ada_w, ada_b):
    tn = 1024
    n = 6 * D_MODEL
    return pl.pallas_call(
        _mod_kernel,
        grid=(DEPTH, n // tn),
        in_specs=[
            pl.BlockSpec((8, D_MODEL), lambda l, j: (0, 0)),
            pl.BlockSpec((1, D_MODEL, tn), lambda l, j: (l, 0, j)),
            pl.BlockSpec((1, 1, tn), lambda l, j: (l, 0, j)),
        ],
        out_specs=pl.BlockSpec((1, 8, tn), lambda l, j: (l, 0, j)),
        out_shape=jax.ShapeDtypeStruct((DEPTH, 8, n), _F32),
        compiler_params=_params(("arbitrary", "arbitrary")),
        name="adaln_mod",
    )(cond, ada_w, ada_b.reshape(DEPTH, 1, n))


def _mod_spec(seq_base, tiles_per_seq):
    if tiles_per_seq is None:
        return pl.BlockSpec((1, 6, D_MODEL), lambda i, *_: (seq_base, 0, 0))
    return pl.BlockSpec((1, 6, D_MODEL), lambda i, *_: (seq_base + i // tiles_per_seq, 0, 0))


def _qkv_kernel(*refs, rope):
    if rope:
        x_ref, m_ref, w_ref, cos_ref, sa_ref, sb_ref, q_ref, k_ref, v_ref = refs
    else:
        x_ref, m_ref, w_ref, q_ref, k_ref, v_ref = refs
    h = (x_ref[...] * (1.0 + m_ref[0, 1:2, :]) + m_ref[0, 0:1, :]).astype(_BF)
    for c, o_ref in enumerate((q_ref, k_ref, v_ref)):
        y = _dot(h, w_ref[:, c * D_MODEL:(c + 1) * D_MODEL])
        if rope and c < 2:
            cos, sa, sb = cos_ref[...], sa_ref[...], sb_ref[...]
            for hd in range(DA_HEADS):
                yh = y[:, hd * 128:(hd + 1) * 128]
                yh = yh * cos + pltpu.roll(yh, 112, 1) * sa + pltpu.roll(yh, 16, 1) * sb
                if c == 0:
                    yh = yh * _Q_SCALE
                o_ref[:, hd * 128:(hd + 1) * 128] = yh.astype(o_ref.dtype)
        else:
            if c == 0:
                y = y * _Q_SCALE
            o_ref[...] = y.astype(o_ref.dtype)


def _rope_tables(seq_len):
    t = jnp.arange(seq_len)
    row = (t // GRID_W).astype(_F32)
    col = (t % GRID_W).astype(_F32)
    lane = jnp.arange(128)
    d = lane % DA_DK
    half = DA_DK // 2
    nf = half // 2
    dd = d % half
    f = dd % nf
    odd = (dd // nf) == 1
    inv = ROPE_THETA ** (-jnp.arange(nf, dtype=_F32) / nf)
    pos = jnp.where((d < half)[None, :], row[:, None], col[:, None])
    ang = pos * inv[f][None, :]
    cos, sin = jnp.cos(ang), jnp.sin(ang)
    sa = jnp.where(odd[None, :], 0.0, -sin)
    sb = jnp.where(odd[None, :], sin, 0.0)
    return cos, sa, sb


def _qkv_proj(x2d, mods, w_bf, *, seq_len, seq_base, rope, kv_dtype, tm):
    rows = x2d.shape[0]
    tps = seq_len // tm if seq_base else None
    in_specs = [
        pl.BlockSpec((tm, D_MODEL), lambda i: (i, 0)),
        _mod_spec(seq_base, tps),
        pl.BlockSpec((D_MODEL, 3 * D_MODEL), lambda i: (0, 0)),
    ]
    args = [x2d, mods, w_bf]
    if rope:
        tabs = _rope_tables(seq_len)
        in_specs += [pl.BlockSpec((tm, 128), lambda i: (i % (seq_len // tm), 0))] * 3
        args += list(tabs)
    out_spec = pl.BlockSpec((tm, D_MODEL), lambda i: (i, 0))
    return pl.pallas_call(
        functools.partial(_qkv_kernel, rope=rope),
        grid=(rows // tm,),
        in_specs=in_specs,
        out_specs=[out_spec, out_spec, out_spec],
        out_shape=[jax.ShapeDtypeStruct((rows, D_MODEL), _BF),
                   jax.ShapeDtypeStruct((rows, D_MODEL), kv_dtype),
                   jax.ShapeDtypeStruct((rows, D_MODEL), kv_dtype)],
        compiler_params=_params(("arbitrary",)),
        name="da_qkv_rope" if rope else "da_qkv",
    )(*args)


def _group_rows(x):
    return x.reshape(x.shape[0] // 8, 8, x.shape[1])


def _attn_kernel(*refs, n_new, tk, lam_init, heads):
    if n_new:
        q_ref, ka_ref, va_ref, kb_ref, vb_ref, lam_ref, sub_ref, o_ref, vta_ref, vtb_ref, s_ref = refs
    else:
        q_ref, ka_ref, va_ref, lam_ref, sub_ref, o_ref, vta_ref, s_ref = refs
    la = ka_ref.shape[1]
    lanes = lambda hd: slice(hd * 128, (hd + 1) * 128)

    @pl.when(pl.program_id(2) == 0)
    def _():
        for hd in range(heads):
            vta_ref[hd] = va_ref[0, :, lanes(hd)].astype(_F32).T.astype(_BF)
            for j in range(n_new):
                vtb_ref[hd, j] = vb_ref[0, j * tk:(j + 1) * tk, lanes(hd)].astype(_F32).T.astype(_BF)

    lf = lam_ref[...]
    lam = (jnp.exp(jnp.sum(lf[0:1] * lf[1:2], axis=1, keepdims=True))
           - jnp.exp(jnp.sum(lf[2:3] * lf[3:4], axis=1, keepdims=True)) + lam_init)

    for hd in range(heads):
        q = q_ref[0, :, lanes(hd)]
        first_map = lax.broadcasted_iota(jnp.int32, q.shape, 1) < DA_DK
        qm = (jnp.where(first_map, q, jnp.zeros_like(q)), jnp.where(first_map, jnp.zeros_like(q), q))

        tiles = [(0, la, lambda: ka_ref[0, :, lanes(hd)].astype(_BF), lambda: vta_ref[hd])]
        for j in range(n_new):
            tiles.append((la + j * tk, tk, lambda j=j: kb_ref[0, j * tk:(j + 1) * tk, lanes(hd)],
                          lambda j=j: vtb_ref[hd, j]))

        def scores(mp, tile, mrun):
            off, rows, load_k, _ = tile
            s = _dot_nt(load_k(), qm[mp])
            s_ref[hd, mp, off:off + rows, :] = s
            m = jnp.max(_group_rows(s), axis=0)
            return m if mrun is None else jnp.maximum(mrun, m)

        def values(mp, tile, mx, state):
            off, rows, _, load_vt = tile
            e = jnp.exp2(s_ref[hd, mp, off:off + rows, :] - mx)
            pv = _dot(load_vt(), e.astype(_BF))
            ls = jnp.sum(_group_rows(e), axis=0)
            return (pv, ls) if state is None else (state[0] + pv, state[1] + ls)

        m0 = m1 = st0 = st1 = None
        for tl in tiles:
            m0 = scores(0, tl, m0)
            m1 = scores(1, tl, m1)
        mx0 = jnp.max(m0, axis=0, keepdims=True)
        mx1 = jnp.max(m1, axis=0, keepdims=True)
        for tl in tiles:
            st0 = values(0, tl, mx0, st0)
            st1 = values(1, tl, mx1, st1)
        l0 = jnp.sum(st0[1], axis=0, keepdims=True)
        l1 = jnp.sum(st1[1], axis=0, keepdims=True)

        o = st0[0] * (1.0 / l0) - st1[0] * (lam / l1)
        ms = jnp.mean(o * o, axis=0, keepdims=True)
        o = o * lax.rsqrt(ms + EPS) * (sub_ref[...] * (1.0 - lam_init))
        o_ref[0, :, lanes(hd)] = o.T.astype(o_ref.dtype)


def _diff_attention(q, ka, va, kb, vb, lam, subln, *, lam_init, tq, tk, heads):
    b, lq, _ = q.shape
    la = ka.shape[1]
    n_new = 0 if kb is None else kb.shape[1] // tk
    width = 128 * heads
    hspec = lambda rows: pl.BlockSpec((1, rows, width), lambda bi, hi, qi: (bi, 0, hi))
    in_specs = [pl.BlockSpec((1, tq, width), lambda bi, hi, qi: (bi, qi, hi)), hspec(la), hspec(la)]
    args = [q, ka, va]
    scratch = [pltpu.VMEM((heads, 128, la), _BF)]
    if n_new:
        in_specs += [hspec(kb.shape[1]), hspec(kb.shape[1])]
        args += [kb, vb]
        scratch.append(pltpu.VMEM((heads, n_new, 128, tk), _BF))
    in_specs += [pl.BlockSpec((4, DA_DK), lambda bi, hi, qi: (0, 0)),
                 pl.BlockSpec((DA_DV, 1), lambda bi, hi, qi: (0, 0))]
    args += [lam, subln.reshape(DA_DV, 1)]
    lk = la + (kb.shape[1] if n_new else 0)
    scratch.append(pltpu.VMEM((heads, 2, lk, tq), _F32))
    return pl.pallas_call(
        functools.partial(_attn_kernel, n_new=n_new, tk=tk, lam_init=lam_init, heads=heads),
        grid=(b, DA_HEADS // heads, lq // tq),
        in_specs=in_specs,
        out_specs=pl.BlockSpec((1, tq, width), lambda bi, hi, qi: (bi, qi, hi)),
        out_shape=jax.ShapeDtypeStruct((b, lq, D_MODEL), _BF),
        scratch_shapes=scratch,
        compiler_params=_params(("arbitrary", "arbitrary", "arbitrary")),
        name="diff_attn_ctx" if n_new else "diff_attn",
    )(*args)


def _exact_zero_like(x):
    u = lax.bitcast_convert_type(x, jnp.uint32)
    u = lax.shift_right_logical(lax.shift_right_logical(u, jnp.uint32(16)), jnp.uint32(16))
    return lax.bitcast_convert_type(u, _F32)


def _attn_ctx_kernel(q_ref, ka_ref, kb_ref, va_ref, vb_ref, lam_ref, sub_ref, o_ref,
                     vta_ref, vtb_ref, s0_ref, s1_ref, mx0_ref, mx1_ref, *, nq, tk, lam_init):
    t = pl.program_id(0)
    la = ka_ref.shape[1]
    n_new = kb_ref.shape[1] // tk

    @pl.when(t == 0)
    def _():
        s1_ref[...] = jnp.zeros(s1_ref.shape, _F32)
        mx1_ref[...] = jnp.zeros(mx1_ref.shape, _F32)

    @pl.when(jnp.maximum(t - 1, 0) % nq == 0)
    def _():
        vta_ref[...] = va_ref[0].astype(_F32).T.astype(_BF)
        for j in range(n_new):
            vtb_ref[j] = vb_ref[0, j * tk:(j + 1) * tk, :].astype(_F32).T.astype(_BF)

    lf = lam_ref[...]
    lam = (jnp.exp(jnp.sum(lf[0:1] * lf[1:2], axis=1, keepdims=True))
           - jnp.exp(jnp.sum(lf[2:3] * lf[3:4], axis=1, keepdims=True)) + lam_init)

    tiles = [(0, la, lambda: ka_ref[0].astype(_BF), lambda: vta_ref[...])]
    for j in range(n_new):
        tiles.append((la + j * tk, tk, lambda j=j: kb_ref[0, j * tk:(j + 1) * tk, :], lambda j=j: vtb_ref[j]))

    def step(s_w, mx_w, s_r, mx_r):
        q = q_ref[0]
        first_map = lax.broadcasted_iota(jnp.int32, q.shape, 1) < DA_DK
        qm = (jnp.where(first_map, q, jnp.zeros_like(q)), jnp.where(first_map, jnp.zeros_like(q), q))
        mx = [jnp.max(mx_r[mp], axis=0, keepdims=True) for mp in range(2)]
        mrun, acc, lsum = [None, None], [None, None], [None, None]
        pace = None
        for off, rows, load_k, load_vt in tiles:
            kt = load_k()
            if pace is not None:
                kt = kt + pace
            for mp in range(2):
                s = _dot_nt(kt, qm[mp])
                s_w[mp, off:off + rows, :] = s
                m = jnp.max(_group_rows(s), axis=0)
                mrun[mp] = m if mrun[mp] is None else jnp.maximum(mrun[mp], m)
            for mp in range(2):
                e = jnp.exp2(s_r[mp, off:off + rows, :] - mx[mp])
                pv = _dot(load_vt(), e.astype(_BF))
                ls = jnp.sum(_group_rows(e), axis=0)
                acc[mp] = pv if acc[mp] is None else acc[mp] + pv
                lsum[mp] = ls if lsum[mp] is None else lsum[mp] + ls
            pace = _exact_zero_like(ls[0:1, 0:128]).astype(_BF)
        for mp in range(2):
            mx_w[mp] = mrun[mp]
        l0 = jnp.sum(lsum[0], axis=0, keepdims=True)
        l1 = jnp.sum(lsum[1], axis=0, keepdims=True)
        o = acc[0] * (1.0 / l0) - acc[1] * (lam / l1)
        ms = jnp.mean(o * o, axis=0, keepdims=True)
        o = o * lax.rsqrt(ms + EPS) * (sub_ref[...] * (1.0 - lam_init))
        o_ref[0] = o.T.astype(o_ref.dtype)

    @pl.when(t % 2 == 0)
    def _():
        step(s0_ref, mx0_ref, s1_ref, mx1_ref)

    @pl.when(t % 2 == 1)
    def _():
        step(s1_ref, mx1_ref, s0_ref, mx0_ref)


def _diff_attention_ctx(q, ka, va, kb, vb, lam, subln, *, lam_init, tq, tk):
    b, lq, _ = q.shape
    la, lb = ka.shape[1], kb.shape[1]
    nq = lq // tq
    units = b * DA_HEADS * nq

    def unit_index(u):
        bh = u // nq
        return bh // DA_HEADS, bh % DA_HEADS, u % nq

    def score_unit(t):
        return unit_index(jnp.minimum(t, units - 1))

    def value_unit(t):
        return unit_index(jnp.maximum(t - 1, 0))

    def q_map(t):
        bi, hi, qi = score_unit(t)
        return bi, qi, hi

    def k_map(t):
        bi, hi, _ = score_unit(t)
        return bi, 0, hi

    def v_map(t):
        bi, hi, _ = value_unit(t)
        return bi, 0, hi

    def o_map(t):
        bi, hi, qi = value_unit(t)
        return bi, qi, hi

    score_buf = pltpu.VMEM((2, la + lb, tq), _F32)
    max_buf = pltpu.VMEM((2, 8, tq), _F32)
    return pl.pallas_call(
        functools.partial(_attn_ctx_kernel, nq=nq, tk=tk, lam_init=lam_init),
        grid=(units + 1,),
        in_specs=[pl.BlockSpec((1, tq, 128), q_map),
                  pl.BlockSpec((1, la, 128), k_map), pl.BlockSpec((1, lb, 128), k_map),
                  pl.BlockSpec((1, la, 128), v_map), pl.BlockSpec((1, lb, 128), v_map),
                  pl.BlockSpec((4, DA_DK), lambda t: (0, 0)), pl.BlockSpec((DA_DV, 1), lambda t: (0, 0))],
        out_specs=pl.BlockSpec((1, tq, 128), o_map),
        out_shape=jax.ShapeDtypeStruct((b, lq, D_MODEL), _BF),
        scratch_shapes=[pltpu.VMEM((128, la), _BF), pltpu.VMEM((lb // tk, 128, tk), _BF),
                        score_buf, score_buf, max_buf, max_buf],
        compiler_params=_params(("arbitrary",)),
        name="diff_attn_ctx",
    )(q, ka, kb, va, vb, lam, subln.reshape(DA_DV, 1))


def _halo_specs(rows, tm, halo):
    assert tm % halo == 0 and rows % tm == 0
    per = tm // halo
    nblk = rows // halo
    main = pl.BlockSpec((tm, D_MODEL), lambda i, *_: (i, 0))
    prev = pl.BlockSpec((halo, D_MODEL), lambda i, *_: (jnp.maximum(i * per - 1, 0), 0))
    nxt = pl.BlockSpec((halo, D_MODEL), lambda i, *_: (jnp.minimum((i + 1) * per, nblk - 1), 0))
    return [prev, main, nxt]


def _seq_keep_flags(tm, seq_len):
    tiles = seq_len // tm
    t = pl.program_id(0) % tiles
    return jnp.where(t != 0, 1.0, 0.0), jnp.where(t != tiles - 1, 1.0, 0.0)


def _modulated_ext(prev, main, nxt, shift, scale, seq_len):
    keep_prev, keep_next = _seq_keep_flags(main.shape[0], seq_len)
    mod = lambda r: r * (1.0 + scale) + shift
    main = mod(main)
    ext = jnp.concatenate([mod(prev) * keep_prev, main, mod(nxt) * keep_next], axis=0)
    return ext.astype(_BF), main.astype(_BF)


def _conv3(u, w, b, tm):
    rows = u.shape[0]
    halo = (rows - tm) // 2
    up = pltpu.roll(u, 1, 0)[halo:halo + tm]
    un = pltpu.roll(u, rows - 1, 0)[halo:halo + tm]
    return up * w[0:1, :] + u[halo:halo + tm] * w[1:2, :] + un * w[2:3, :] + b


def _resident(shape):
    return pl.BlockSpec(shape, lambda i: (0,) * len(shape), pipeline_mode=pl.Buffered(1))


def _ext(refs):
    return jnp.concatenate([r[...] for r in refs], axis=0)


def _mix_ffn_kernel(*refs, seq_len, mlstm):
    refs = list(refs)
    n_act = 3 if mlstm else 1
    acts = [refs[3 * i:3 * i + 3] for i in range(n_act)]
    x_refs = refs[3 * n_act:3 * n_act + 3]
    rest = refs[3 * n_act + 3:]
    if mlstm:
        nw_ref, rest = rest[0], rest[1:]
    m_ref, wo_ref, g1_ref, b1_ref, wu_ref, cw_ref, cb_ref, wd_ref, g2_ref, b2_ref, o_ref, a_ref = rest
    tm = x_refs[1].shape[0]
    halo = x_refs[0].shape[0]

    if mlstm:
        hsum = _ext(acts[0]) + _ext(acts[1])
        parts = []
        for hd in range(ML_HEADS):
            hh = hsum[:, hd * ML_DV:(hd + 1) * ML_DV]
            mu = jnp.mean(hh, axis=-1, keepdims=True)
            hc = hh - mu
            var = jnp.mean(hc * hc, axis=-1, keepdims=True)
            parts.append(hc * lax.rsqrt(var + EPS))
        hn = jnp.concatenate(parts, axis=-1) * nw_ref[...]
        act = (_ext(acts[2]).astype(_F32) * hn).astype(_BF)
    else:
        act = _ext(acts[0])
    z1 = ALPHA * _ext(x_refs) + m_ref[0, 2:3, :] * _dot(act, wo_ref[...])
    x1 = _layer_norm_rows(z1, g1_ref[...], b1_ref[...])

    h, _ = _modulated_ext(x1[0:halo], x1[halo:halo + tm], x1[halo + tm:], m_ref[0, 3:4, :], m_ref[0, 4:5, :],
                          seq_len)
    for j in range(D_FF // _FF_CHUNK):
        g0, v0 = j * _FF_CHUNK, D_FF + j * _FF_CHUNK
        gate = _conv3(_dot(h, wu_ref[:, g0:g0 + _FF_CHUNK]), cw_ref[:, g0:g0 + _FF_CHUNK],
                      cb_ref[:, g0:g0 + _FF_CHUNK], tm)
        val = _conv3(_dot(h, wu_ref[:, v0:v0 + _FF_CHUNK]), cw_ref[:, v0:v0 + _FF_CHUNK],
                     cb_ref[:, v0:v0 + _FF_CHUNK], tm)
        a_ref[:, g0:g0 + _FF_CHUNK] = (_silu(gate) * val).astype(_BF)
    z2 = ALPHA * x1[halo:halo + tm] + m_ref[0, 5:6, :] * _dot(a_ref[...], wd_ref[...])
    o_ref[...] = _layer_norm_rows(z2, g2_ref[...], b2_ref[...])


def _mix_ffn(acts, w_o_bf, x2d, mods, ln_g, ln_b, w_up_bf, conv_w, conv_b, w_down_bf, *,
             seq_len, seq_base, tm, norm_w=None):
    rows = x2d.shape[0]
    mlstm = norm_w is not None
    tps = seq_len // tm if seq_base else None
    vec_spec = pl.BlockSpec((1, D_MODEL), lambda i: (0, 0))
    vec = lambda a: a.reshape(1, D_MODEL)
    in_specs, args = [], []
    for arr in list(acts) + [x2d]:
        in_specs += _halo_specs(rows, tm, _MIX_HALO)
        args += [arr, arr, arr]
    if mlstm:
        in_specs.append(vec_spec)
        args.append(vec(norm_w))
    in_specs += [_mod_spec(seq_base, tps), _resident((D_MODEL, D_MODEL)), vec_spec, vec_spec,
                 _resident((D_MODEL, 2 * D_FF)), _resident((3, 2 * D_FF)), _resident((1, 2 * D_FF)),
                 _resident((D_FF, D_MODEL)), vec_spec, vec_spec]
    args += [mods, w_o_bf, vec(ln_g[0]), vec(ln_b[0]), w_up_bf, conv_w, conv_b.reshape(1, 2 * D_FF), w_down_bf,
             vec(ln_g[1]), vec(ln_b[1])]
    return pl.pallas_call(
        functools.partial(_mix_ffn_kernel, seq_len=seq_len, mlstm=mlstm),
        grid=(rows // tm,),
        in_specs=in_specs,
        out_specs=pl.BlockSpec((tm, D_MODEL), lambda i: (i, 0)),
        out_shape=jax.ShapeDtypeStruct((rows, D_MODEL), _F32),
        scratch_shapes=[pltpu.VMEM((tm, D_FF), _BF)],
        compiler_params=_params(("arbitrary",)),
        name="ml_out_ffn" if mlstm else "da_out_ffn",
    )(*args)


def _mlproj_kernel(prev_ref, x_ref, next_ref, m_ref, w_ref, wgt_ref, cw_ref, cb_ref, bgt_ref,
                   q_ref, k_ref, v_ref, og_ref, gtt_ref, *, seq_len):
    tm = x_ref.shape[0]
    shift, scale = m_ref[0, 0:1, :], m_ref[0, 1:2, :]
    h, hm = _modulated_ext(prev_ref[...], x_ref[...], next_ref[...], shift, scale, seq_len)
    ch = _FF_CHUNK
    v0, o0 = 2 * ML_QK, 2 * ML_QK + ML_V
    pace = None
    for i in range(2 * ML_QK // ch):
        lo = i * ch
        cols = slice(lo, lo + ch)
        u = _dot(h, w_ref[:, cols])
        if pace is not None:
            u = u + pace
        y = _silu(_conv3(u, cw_ref[:, cols], cb_ref[:, cols], tm))
        if lo < ML_QK:
            q_ref[:, cols] = y.astype(_BF)
        else:
            k_ref[:, lo - ML_QK:lo - ML_QK + ch] = (y * (ML_DK ** -0.5)).astype(_BF)
        vc = _dot(hm, w_ref[:, v0 + lo:v0 + lo + ch])
        oc = _dot(hm, w_ref[:, o0 + lo:o0 + lo + ch])
        v_ref[:, cols] = vc.astype(_BF)
        og_ref[:, cols] = _sigmoid(oc).astype(_BF)
        pace = _exact_zero_like(vc[0:1, :]) + _exact_zero_like(oc[0:1, :])
    wg_hi, wg_lo = _split2(wgt_ref[...])
    gtt_ref[...] = _dot_nt(wg_hi, hm) + _dot_nt(wg_lo, hm) + bgt_ref[...]


def _ml_proj(x2d, mods, w_main_bf, w_gate, conv_w, conv_b, b_gate, *, seq_len, seq_base, tm):
    rows = x2d.shape[0]
    tps = seq_len // tm if seq_base else None
    prev, main, nxt = _halo_specs(rows, tm, _HALO)
    n_main = 2 * ML_QK + 2 * ML_V
    full = lambda shape: pl.BlockSpec(shape, lambda i: (0,) * len(shape))
    row = lambda n: pl.BlockSpec((tm, n), lambda i: (i, 0))
    return pl.pallas_call(
        functools.partial(_mlproj_kernel, seq_len=seq_len),
        grid=(rows // tm,),
        in_specs=[prev, main, nxt, _mod_spec(seq_base, tps), full((D_MODEL, n_main)),
                  full((ML_GATES, D_MODEL)), full((3, 2 * ML_QK)), full((1, 2 * ML_QK)), full((ML_GATES, 1))],
        out_specs=[row(ML_QK), row(ML_QK), row(ML_V), row(ML_V),
                   pl.BlockSpec((ML_GATES, tm), lambda i: (0, i))],
        out_shape=[jax.ShapeDtypeStruct((rows, ML_QK), _BF), jax.ShapeDtypeStruct((rows, ML_QK), _BF),
                   jax.ShapeDtypeStruct((rows, ML_V), _BF), jax.ShapeDtypeStruct((rows, ML_V), _BF),
                   jax.ShapeDtypeStruct((ML_GATES, rows), _F32)],
        compiler_params=_params(("arbitrary",)),
        name="ml_inproj_conv",
    )(x2d, x2d, x2d, mods, w_main_bf, w_gate.T, conv_w, conv_b.reshape(1, 2 * ML_QK),
      b_gate.reshape(ML_GATES, 1))


def _mlstm_chunk(q, k, v, i_row, b_col, b_row, total, m_prev, c_prev, n_prev, causal):
    t = q.shape[0]
    wide = lambda col: jnp.broadcast_to(col, (t, 128))
    tile2 = lambda w: jnp.concatenate([w] * (t // 128), axis=1)
    ti = lax.broadcasted_iota(jnp.int32, (t, t), 0)
    si = lax.broadcasted_iota(jnp.int32, (t, t), 1)
    a_row = i_row - b_row
    amat = jnp.where((si <= ti) if causal else (si >= ti), a_row, _NEG)
    g_col = jnp.maximum(m_prev, jnp.max(amat, axis=1, keepdims=True))
    g_w = wide(g_col)
    w_intra = jnp.exp(amat - tile2(g_w))
    w_inter_w = jnp.exp(m_prev - g_w)
    s = _dot_nt(q, k) * w_intra
    den_parts = w_inter_w * (q.astype(_F32) * n_prev)
    for c in range(t // 128):
        den_parts = den_parts + s[:, c * 128:(c + 1) * 128]
    den_col = jnp.sum(den_parts, axis=1, keepdims=True)
    r_col = 1.0 / jnp.maximum(jnp.abs(den_col), jnp.exp(-(b_col + g_col)))
    num = jnp.concatenate([w_inter_w, w_inter_w], axis=1) * _dot(q, c_prev.astype(_BF)) + _dot(s.astype(_BF), v)
    r_w = wide(r_col)
    h = num * jnp.concatenate([r_w, r_w], axis=1)
    g_end = jnp.maximum(m_prev, jnp.max(a_row, axis=1, keepdims=True))
    w_s = jnp.exp(a_row - g_end)
    carry = jnp.exp(m_prev - g_end)
    kw_t = (k.astype(_F32).T * w_s).astype(_BF)
    c_new = carry * c_prev + _dot(kw_t, v)
    ws_hi, ws_lo = _split2(jnp.broadcast_to(w_s, (8, t)))
    n_new = carry * n_prev + (_dot(ws_hi, k) + _dot(ws_lo, k))[0:1, :]
    return h, c_new, n_new, total + g_end


def _mlstm_kernel(*refs, has_init, want_state):
    refs = list(refs)
    (qf_ref, kf_ref, vf_ref, gtf_ref, qb_ref, kb_ref, vb_ref, gtb_ref) = refs[:8]
    pos = 8
    if has_init:
        c0_ref, n0_ref, m0_ref = refs[pos:pos + 3]
        pos += 3
    hf_ref, hb_ref = refs[pos:pos + 2]
    pos += 2
    if want_state:
        co_ref, no_ref, mo_ref = refs[pos:pos + 3]
        pos += 3
    c_ref, n_ref, m_ref = refs[pos:pos + 3]
    ci = pl.program_id(1)
    t = qf_ref.shape[1]

    @pl.when(ci == 0)
    def _():
        if has_init:
            c_ref[...] = c0_ref[0]
            n_ref[...] = n0_ref[0]
            m_ref[...] = jnp.broadcast_to(m0_ref[0], m_ref.shape)
        else:
            c_ref[...] = jnp.zeros(c_ref.shape, _F32)
            n_ref[...] = jnp.zeros(n_ref.shape, _F32)
            m_ref[...] = jnp.zeros(m_ref.shape, _F32)

    ri = lax.broadcasted_iota(jnp.int32, (t, t), 0)
    cj = lax.broadcasted_iota(jnp.int32, (t, t), 1)
    lower = (cj <= ri).astype(_BF)
    upper = (cj >= ri).astype(_BF)

    def cum(mat, xt):
        return sum(_dot_nt(mat, p) for p in _split3(xt))

    def cum_t(xt, mat):
        return sum(_dot(p, mat) for p in _split3(xt))

    for d, (q_ref, k_ref, v_ref, gt_ref, h_ref) in enumerate(
            ((qf_ref, kf_ref, vf_ref, gtf_ref, hf_ref), (qb_ref, kb_ref, vb_ref, gtb_ref, hb_ref))):
        gt = gt_ref[...]
        is_f_row = (lax.broadcasted_iota(jnp.int32, (ML_GATES, 1), 0) % 8) >= 4
        xt = jnp.where(is_f_row, _log_sigmoid(gt), gt)
        if d == 0:
            bc = cum(lower, xt)
            br = cum_t(xt, upper)
        else:
            bc = cum(upper, xt)
            br = cum_t(xt, lower)
        tot = jnp.sum(xt, axis=1, keepdims=True)
        for hd in range(ML_HEADS):
            ic, fc = d * 8 + hd, d * 8 + 4 + hd
            st = d * ML_HEADS + hd
            h, c_new, n_new, m_new = _mlstm_chunk(
                q_ref[0, :, hd * ML_DK:(hd + 1) * ML_DK], k_ref[0, :, hd * ML_DK:(hd + 1) * ML_DK],
                v_ref[0, :, hd * ML_DV:(hd + 1) * ML_DV],
                xt[ic:ic + 1, :], bc[:, fc:fc + 1], br[fc:fc + 1, :],
                tot[fc:fc + 1, :], m_ref[st:st + 1, 0:1], c_ref[st], n_ref[st:st + 1, :], d == 0)
            h_ref[0, :, hd * ML_DV:(hd + 1) * ML_DV] = h
            c_ref[st] = c_new
            n_ref[st:st + 1, :] = n_new
            m_ref[st:st + 1, :] = jnp.broadcast_to(m_new, (1, 128))

    if want_state:
        @pl.when(ci == pl.num_programs(1) - 1)
        def _():
            co_ref[0] = c_ref[...]
            no_ref[0] = n_ref[...]
            mo_ref[0] = m_ref[...]


def _mlstm_scan(q, k, v, gates_t, init, *, want_state):
    b, l, _ = q.shape
    t = _ML_CHUNK
    nc = l // t
    fwd = lambda n: pl.BlockSpec((1, t, n), lambda bi, ci: (bi, ci, 0))
    bwd = lambda n: pl.BlockSpec((1, t, n), lambda bi, ci: (bi, nc - 1 - ci, 0))
    gtf = pl.BlockSpec((ML_GATES, t), lambda bi, ci: (0, bi * nc + ci))
    gtb = pl.BlockSpec((ML_GATES, t), lambda bi, ci: (0, bi * nc + nc - 1 - ci))
    in_specs = [fwd(ML_QK), fwd(ML_QK), fwd(ML_V), gtf, bwd(ML_QK), bwd(ML_QK), bwd(ML_V), gtb]
    args = [q, k, v, gates_t, q, k, v, gates_t]
    has_init = init is not None
    if has_init:
        c0, n0, m0 = init
        in_specs += [pl.BlockSpec((1, 8, ML_DK, ML_DV), lambda bi, ci: (bi, 0, 0, 0)),
                     pl.BlockSpec((1, 8, ML_DK), lambda bi, ci: (bi, 0, 0)),
                     pl.BlockSpec((1, 8, 1), lambda bi, ci: (bi, 0, 0))]
        args += [c0.reshape(b, 8, ML_DK, ML_DV), n0.reshape(b, 8, ML_DK), m0.reshape(b, 8, 1)]
    out_specs = [fwd(ML_V), bwd(ML_V)]
    out_shape = [jax.ShapeDtypeStruct((b, l, ML_V), _F32), jax.ShapeDtypeStruct((b, l, ML_V), _F32)]
    if want_state:
        out_specs += [pl.BlockSpec((1, 8, ML_DK, ML_DV), lambda bi, ci: (bi, 0, 0, 0)),
                      pl.BlockSpec((1, 8, ML_DK), lambda bi, ci: (bi, 0, 0)),
                      pl.BlockSpec((1, 8, 128), lambda bi, ci: (bi, 0, 0))]
        out_shape += [jax.ShapeDtypeStruct((b, 8, ML_DK, ML_DV), _F32),
                      jax.ShapeDtypeStruct((b, 8, ML_DK), _F32),
                      jax.ShapeDtypeStruct((b, 8, 128), _F32)]
    return pl.pallas_call(
        functools.partial(_mlstm_kernel, has_init=has_init, want_state=want_state),
        grid=(b, nc),
        in_specs=in_specs,
        out_specs=out_specs,
        out_shape=out_shape,
        scratch_shapes=[pltpu.VMEM((8, ML_DK, ML_DV), _F32), pltpu.VMEM((8, ML_DK), _F32),
                        pltpu.VMEM((8, 128), _F32)],
        compiler_params=_params(("arbitrary", "arbitrary")),
        name="mlstm_scan_state" if want_state else "mlstm_scan",
    )(*args)


def kernel(x_prompt, x_sample, c, cache_k, cache_v, state_C, state_n, state_m, c_ctx, ada_w, ada_b, ln_g, ln_b,
           da_w_qkv, da_lam, da_subln, da_w_o, ml_w_in, ml_conv_w, ml_conv_b, ml_b_gate, ml_norm_w, ml_w_out,
           ffn_w_up, ffn_conv_w, ffn_conv_b, ffn_w_down):
    bp, lp, _ = x_prompt.shape
    bs, ls, _ = x_sample.shape
    cond = jnp.concatenate([c_ctx[None, :], c, jnp.zeros((8 - 1 - bs, D_MODEL), _F32)], axis=0)
    mods = _modulation(cond, ada_w, ada_b).reshape(DEPTH, 8, 6, D_MODEL)

    xp = x_prompt.reshape(bp * lp, D_MODEL)
    xs = x_sample.reshape(bs * ls, D_MODEL)
    groups = (dict(seq_len=lp, seq_base=0), dict(seq_len=ls, seq_base=1))

    lam_init = 0.8 - 0.6 * math.exp(-0.3 * 0)
    w_qkv = da_w_qkv[0].astype(_BF)
    w_o = da_w_o[0].astype(_BF)
    qp, kp, vp = _qkv_proj(xp, mods[0], w_qkv, rope=False, kv_dtype=_F32, tm=_ROW_TILE, **groups[0])
    qs, ks, vs = _qkv_proj(xs, mods[0], w_qkv, rope=True, kv_dtype=_BF, tm=_ROW_TILE, **groups[1])
    as3 = lambda a, b: a.reshape(b, -1, D_MODEL)
    op = _diff_attention(as3(qp, bp), as3(kp, bp), as3(vp, bp), None, None, da_lam[0], da_subln[0],
                         lam_init=lam_init, tq=256, tk=512, heads=DA_HEADS)
    os_ = _diff_attention_ctx(as3(qs, bs), cache_k[:, 0].reshape(bs, -1, D_MODEL),
                              cache_v[:, 0].reshape(bs, -1, D_MODEL), as3(ks, bs), as3(vs, bs),
                              da_lam[0], da_subln[0], lam_init=lam_init, tq=256, tk=512)
    new_k = kp.reshape(bp, 1, lp, DA_HEADS, 2, DA_DK)
    new_v = vp.reshape(bp, 1, lp, DA_HEADS, DA_DV)

    def mix_ffn(acts, w_mix, x2d, i, grp, **kw):
        return _mix_ffn(acts, w_mix, x2d, mods[i], ln_g[i], ln_b[i], ffn_w_up[i].astype(_BF), ffn_conv_w[i],
                        ffn_conv_b[i], ffn_w_down[i].astype(_BF), tm=min(_ROW_TILE, grp["seq_len"]), **grp, **kw)

    xp = mix_ffn([op.reshape(-1, D_MODEL)], w_o, xp, 0, groups[0])
    xs = mix_ffn([os_.reshape(-1, D_MODEL)], w_o, xs, 0, groups[1])

    n_main = 2 * ML_QK + 2 * ML_V
    w_main = ml_w_in[0][:, :n_main].astype(_BF)
    w_gate = ml_w_in[0][:, n_main:]
    w_out = ml_w_out[0].astype(_BF)
    outs = []
    for x2d, grp, nb, init in ((xp, groups[0], bp, None),
                               (xs, groups[1], bs, (state_C[:, 0], state_n[:, 0], state_m[:, 0]))):
        q, k, v, og, gtt = _ml_proj(x2d, mods[1], w_main, w_gate, ml_conv_w[0], ml_conv_b[0], ml_b_gate[0],
                                        tm=min(_ROW_TILE, grp["seq_len"]), **grp)
        r3 = lambda a: a.reshape(nb, -1, a.shape[-1])
        res = _mlstm_scan(r3(q), r3(k), r3(v), gtt, init, want_state=init is None)
        hf, hb = res[0].reshape(-1, ML_V), res[1].reshape(-1, ML_V)
        outs.append((mix_ffn([hf, hb, og], w_out, x2d, 1, grp, norm_w=ml_norm_w[0]), res[2:]))
    (xp, (c_fin, n_fin, m_fin)), (xs, _) = outs
    new_c = c_fin.reshape(bp, 1, 2, ML_HEADS, ML_DK, ML_DV)
    new_n = n_fin.reshape(bp, 1, 2, ML_HEADS, ML_DK)
    new_m = m_fin[:, :, 0].reshape(bp, 1, 2, ML_HEADS)
    return (xp.reshape(bp, lp, D_MODEL), xs.reshape(bs, ls, D_MODEL), new_k, new_v, new_c, new_n, new_m)
```

```python
import functools
import math

import jax
import jax.numpy as jnp
from jax import lax
from jax.experimental import pallas as pl
from jax.experimental.pallas import tpu as pltpu

D_MODEL = 1024
DEPTH = 2
GRID_W = 64
DA_HEADS = 8
DA_DK = 64
DA_DV = 128
ML_HEADS = 4
ML_DK = 128
ML_DV = 256
ML_QK = ML_HEADS * ML_DK
ML_V = ML_HEADS * ML_DV
ML_GATES = 16
D_FF = 2816
ROPE_THETA = 10000.0
ALPHA = (2 * DEPTH) ** 0.25
EPS = 1e-5

_BF = jnp.bfloat16
_F32 = jnp.float32
_NEG = -1e30

_VMEM_LIMIT_BYTES = 56 * 1024 * 1024
_HALO = 8
_MIX_HALO = 16
_ML_CHUNK = 256
_ROW_TILE = 512
_FF_CHUNK = 256
_Q_SCALE = DA_DK ** -0.5 * math.log2(math.e)


def _dot(a, b):
    return jnp.dot(a, b, preferred_element_type=_F32)


def _dot_nt(a, b):
    return lax.dot_general(a, b, (((1,), (1,)), ((), ())), preferred_element_type=_F32)


def _split2(x):
    hi = x.astype(_BF)
    lo = (x - hi.astype(_F32)).astype(_BF)
    return hi, lo


def _split3(x):
    hi = x.astype(_BF)
    r = x - hi.astype(_F32)
    mid = r.astype(_BF)
    lo = (r - mid.astype(_F32)).astype(_BF)
    return hi, mid, lo


def _dot_f32(a, b):
    ah, al = _split2(a)
    bh, bl = _split2(b)
    return _dot(ah, bh) + _dot(al, bh) + _dot(ah, bl)


def _sigmoid(x):
    return 1.0 / (1.0 + jnp.exp(-x))


def _silu(x):
    return x * _sigmoid(x)


def _log_sigmoid(x):
    return jnp.minimum(x, 0.0) - jnp.log(1.0 + jnp.exp(-jnp.abs(x)))


def _layer_norm_rows(z, g, b):
    mu = jnp.mean(z, axis=-1, keepdims=True)
    zc = z - mu
    var = jnp.mean(zc * zc, axis=-1, keepdims=True)
    return zc * lax.rsqrt(var + EPS) * g + b


def _params(sem):
    return pltpu.CompilerParams(dimension_semantics=sem, vmem_limit_bytes=_VMEM_LIMIT_BYTES)


def _mod_kernel(c_ref, w_ref, b_ref, o_ref):
    s = _silu(c_ref[...])
    o_ref[0] = _dot_f32(s, w_ref[0]) + b_ref[0]


def _modulation(cond, ada_w, ada_b):
    tn = 1024
    n = 6 * D_MODEL
    return pl.pallas_call(
        _mod_kernel,
        grid=(DEPTH, n // tn),
        in_specs=[
            pl.BlockSpec((8, D_MODEL), lambda l, j: (0, 0)),
            pl.BlockSpec((1, D_MODEL, tn), lambda l, j: (l, 0, j)),
            pl.BlockSpec((1, 1, tn), lambda l, j: (l, 0, j)),
        ],
        out_specs=pl.BlockSpec((1, 8, tn), lambda l, j: (l, 0, j)),
        out_shape=jax.ShapeDtypeStruct((DEPTH, 8, n), _F32),
        compiler_params=_params(("arbitrary", "arbitrary")),
        name="adaln_mod",
    )(cond, ada_w, ada_b.reshape(DEPTH, 1, n))


def _mod_spec(seq_base, tiles_per_seq):
    if tiles_per_seq is None:
        return pl.BlockSpec((1, 6, D_MODEL), lambda i, *_: (seq_base, 0, 0))
    return pl.BlockSpec((1, 6, D_MODEL), lambda i, *_: (seq_base + i // tiles_per_seq, 0, 0))


def _qkv_kernel(*refs, rope):
    if rope:
        x_ref, m_ref, w_ref, cos_ref, sa_ref, sb_ref, q_ref, k_ref, v_ref = refs
    else:
        x_ref, m_ref, w_ref, q_ref, k_ref, v_ref = refs
    h = (x_ref[...] * (1.0 + m_ref[0, 1:2, :]) + m_ref[0, 0:1, :]).astype(_BF)
    for c, o_ref in enumerate((q_ref, k_ref, v_ref)):
        y = _dot(h, w_ref[:, c * D_MODEL:(c + 1) * D_MODEL])
        if rope and c < 2:
            cos, sa, sb = cos_ref[...], sa_ref[...], sb_ref[...]
            for hd in range(DA_HEADS):
                yh = y[:, hd * 128:(hd + 1) * 128]
                yh = yh * cos + pltpu.roll(yh, 112, 1) * sa + pltpu.roll(yh, 16, 1) * sb
                if c == 0:
                    yh = yh * _Q_SCALE
                o_ref[:, hd * 128:(hd + 1) * 128] = yh.astype(o_ref.dtype)
        else:
            if c == 0:
                y = y * _Q_SCALE
            o_ref[...] = y.astype(o_ref.dtype)


def _rope_tables(seq_len):
    t = jnp.arange(seq_len)
    row = (t // GRID_W).astype(_F32)
    col = (t % GRID_W).astype(_F32)
    lane = jnp.arange(128)
    d = lane % DA_DK
    half = DA_DK // 2
    nf = half // 2
    dd = d % half
    f = dd % nf
    odd = (dd // nf) == 1
    inv = ROPE_THETA ** (-jnp.arange(nf, dtype=_F32) / nf)
    pos = jnp.where((d < half)[None, :], row[:, None], col[:, None])
    ang = pos * inv[f][None, :]
    cos, sin = jnp.cos(ang), jnp.sin(ang)
    sa = jnp.where(odd[None, :], 0.0, -sin)
    sb = jnp.where(odd[None, :], sin, 0.0)
    return cos, sa, sb


def _qkv_proj(x2d, mods, w_bf, *, seq_len, seq_base, rope, kv_dtype, tm):
    rows = x2d.shape[0]
    tps = seq_len // tm if seq_base else None
    in_specs = [
        pl.BlockSpec((tm, D_MODEL), lambda i: (i, 0)),
        _mod_spec(seq_base, tps),
        pl.BlockSpec((D_MODEL, 3 * D_MODEL), lambda i: (0, 0)),
    ]
    args = [x2d, mods, w_bf]
    if rope:
        tabs = _rope_tables(seq_len)
        in_specs += [pl.BlockSpec((tm, 128), lambda i: (i % (seq_len // tm), 0))] * 3
        args += list(tabs)
    out_spec = pl.BlockSpec((tm, D_MODEL), lambda i: (i, 0))
    return pl.pallas_call(
        functools.partial(_qkv_kernel, rope=rope),
        grid=(rows // tm,),
        in_specs=in_specs,
        out_specs=[out_spec, out_spec, out_spec],
        out_shape=[jax.ShapeDtypeStruct((rows, D_MODEL), _BF),
                   jax.ShapeDtypeStruct((rows, D_MODEL), kv_dtype),
                   jax.ShapeDtypeStruct((rows, D_MODEL), kv_dtype)],
        compiler_params=_params(("arbitrary",)),
        name="da_qkv_rope" if rope else "da_qkv",
    )(*args)


def _group_rows(x):
    return x.reshape(x.shape[0] // 8, 8, x.shape[1])


def _attn_kernel(*refs, n_new, tk, lam_init, heads):
    if n_new:
        q_ref, ka_ref, va_ref, kb_ref, vb_ref, lam_ref, sub_ref, o_ref, vta_ref, vtb_ref, s_ref = refs
    else:
        q_ref, ka_ref, va_ref, lam_ref, sub_ref, o_ref, vta_ref, s_ref = refs
    la = ka_ref.shape[1]
    lanes = lambda hd: slice(hd * 128, (hd + 1) * 128)

    @pl.when(pl.program_id(2) == 0)
    def _():
        for hd in range(heads):
            vta_ref[hd] = va_ref[0, :, lanes(hd)].astype(_F32).T.astype(_BF)
            for j in range(n_new):
                vtb_ref[hd, j] = vb_ref[0, j * tk:(j + 1) * tk, lanes(hd)].astype(_F32).T.astype(_BF)

    lf = lam_ref[...]
    lam = (jnp.exp(jnp.sum(lf[0:1] * lf[1:2], axis=1, keepdims=True))
           - jnp.exp(jnp.sum(lf[2:3] * lf[3:4], axis=1, keepdims=True)) + lam_init)

    for hd in range(heads):
        q = q_ref[0, :, lanes(hd)]
        first_map = lax.broadcasted_iota(jnp.int32, q.shape, 1) < DA_DK
        qm = (jnp.where(first_map, q, jnp.zeros_like(q)), jnp.where(first_map, jnp.zeros_like(q), q))

        tiles = [(0, la, lambda: ka_ref[0, :, lanes(hd)].astype(_BF), lambda: vta_ref[hd])]
        for j in range(n_new):
            tiles.append((la + j * tk, tk, lambda j=j: kb_ref[0, j * tk:(j + 1) * tk, lanes(hd)],
                          lambda j=j: vtb_ref[hd, j]))

        def scores(mp, tile, mrun):
            off, rows, load_k, _ = tile
            s = _dot_nt(load_k(), qm[mp])
            s_ref[hd, mp, off:off + rows, :] = s
            m = jnp.max(_group_rows(s), axis=0)
            return m if mrun is None else jnp.maximum(mrun, m)

        def values(mp, tile, mx, state):
            off, rows, _, load_vt = tile
            e = jnp.exp2(s_ref[hd, mp, off:off + rows, :] - mx)
            pv = _dot(load_vt(), e.astype(_BF))
            ls = jnp.sum(_group_rows(e), axis=0)
            return (pv, ls) if state is None else (state[0] + pv, state[1] + ls)

        m0 = m1 = st0 = st1 = None
        for tl in tiles:
            m0 = scores(0, tl, m0)
            m1 = scores(1, tl, m1)
        mx0 = jnp.max(m0, axis=0, keepdims=True)
        mx1 = jnp.max(m1, axis=0, keepdims=True)
        for tl in tiles:
            st0 = values(0, tl, mx0, st0)
            st1 = values(1, tl, mx1, st1)
        l0 = jnp.sum(st0[1], axis=0, keepdims=True)
        l1 = jnp.sum(st1[1], axis=0, keepdims=True)

        o = st0[0] * (1.0 / l0) - st1[0] * (lam / l1)
        ms = jnp.mean(o * o, axis=0, keepdims=True)
        o = o * lax.rsqrt(ms + EPS) * (sub_ref[...] * (1.0 - lam_init))
        o_ref[0, :, lanes(hd)] = o.T.astype(o_ref.dtype)


def _diff_attention(q, ka, va, kb, vb, lam, subln, *, lam_init, tq, tk, heads):
    b, lq, _ = q.shape
    la = ka.shape[1]
    n_new = 0 if kb is None else kb.shape[1] // tk
    width = 128 * heads
    hspec = lambda rows: pl.BlockSpec((1, rows, width), lambda bi, hi, qi: (bi, 0, hi))
    in_specs = [pl.BlockSpec((1, tq, width), lambda bi, hi, qi: (bi, qi, hi)), hspec(la), hspec(la)]
    args = [q, ka, va]
    scratch = [pltpu.VMEM((heads, 128, la), _BF)]
    if n_new:
        in_specs += [hspec(kb.shape[1]), hspec(kb.shape[1])]
        args += [kb, vb]
        scratch.append(pltpu.VMEM((heads, n_new, 128, tk), _BF))
    in_specs += [pl.BlockSpec((4, DA_DK), lambda bi, hi, qi: (0, 0)),
                 pl.BlockSpec((DA_DV, 1), lambda bi, hi, qi: (0, 0))]
    args += [lam, subln.reshape(DA_DV, 1)]
    lk = la + (kb.shape[1] if n_new else 0)
    scratch.append(pltpu.VMEM((heads, 2, lk, tq), _F32))
    return pl.pallas_call(
        functools.partial(_attn_kernel, n_new=n_new, tk=tk, lam_init=lam_init, heads=heads),
        grid=(b, DA_HEADS // heads, lq // tq),
        in_specs=in_specs,
        out_specs=pl.BlockSpec((1, tq, width), lambda bi, hi, qi: (bi, qi, hi)),
        out_shape=jax.ShapeDtypeStruct((b, lq, D_MODEL), _BF),
        scratch_shapes=scratch,
        compiler_params=_params(("arbitrary", "arbitrary", "arbitrary")),
        name="diff_attn_ctx" if n_new else "diff_attn",
    )(*args)


def _exact_zero_like(x):
    u = lax.bitcast_convert_type(x, jnp.uint32)
    u = lax.shift_right_logical(lax.shift_right_logical(u, jnp.uint32(16)), jnp.uint32(16))
    return lax.bitcast_convert_type(u, _F32)


def _attn_ctx_kernel(q_ref, ka_ref, kb_ref, va_ref, vb_ref, lam_ref, sub_ref, o_ref,
                     vta_ref, vtb_ref, s0_ref, s1_ref, mx0_ref, mx1_ref, *, nq, tk, lam_init):
    t = pl.program_id(0)
    la = ka_ref.shape[1]
    n_new = kb_ref.shape[1] // tk

    @pl.when(t == 0)
    def _():
        s1_ref[...] = jnp.zeros(s1_ref.shape, _F32)
        mx1_ref[...] = jnp.zeros(mx1_ref.shape, _F32)

    @pl.when(jnp.maximum(t - 1, 0) % nq == 0)
    def _():
        vta_ref[...] = va_ref[0].astype(_F32).T.astype(_BF)
        for j in range(n_new):
            vtb_ref[j] = vb_ref[0, j * tk:(j + 1) * tk, :].astype(_F32).T.astype(_BF)

    lf = lam_ref[...]
    lam = (jnp.exp(jnp.sum(lf[0:1] * lf[1:2], axis=1, keepdims=True))
           - jnp.exp(jnp.sum(lf[2:3] * lf[3:4], axis=1, keepdims=True)) + lam_init)

    tiles = [(0, la, lambda: ka_ref[0].astype(_BF), lambda: vta_ref[...])]
    for j in range(n_new):
        tiles.append((la + j * tk, tk, lambda j=j: kb_ref[0, j * tk:(j + 1) * tk, :], lambda j=j: vtb_ref[j]))

    def step(s_w, mx_w, s_r, mx_r):
        q = q_ref[0]
        first_map = lax.broadcasted_iota(jnp.int32, q.shape, 1) < DA_DK
        qm = (jnp.where(first_map, q, jnp.zeros_like(q)), jnp.where(first_map, jnp.zeros_like(q), q))
        mx = [jnp.max(mx_r[mp], axis=0, keepdims=True) for mp in range(2)]
        mrun, acc, lsum = [None, None], [None, None], [None, None]
        pace = None
        for off, rows, load_k, load_vt in tiles:
            kt = load_k()
            if pace is not None:
                kt = kt + pace
            for mp in range(2):
                s = _dot_nt(kt, qm[mp])
                s_w[mp, off:off + rows, :] = s
                m = jnp.max(_group_rows(s), axis=0)
                mrun[mp] = m if mrun[mp] is None else jnp.maximum(mrun[mp], m)
            for mp in range(2):
                e = jnp.exp2(s_r[mp, off:off + rows, :] - mx[mp])
                pv = _dot(load_vt(), e.astype(_BF))
                ls = jnp.sum(_group_rows(e), axis=0)
                acc[mp] = pv if acc[mp] is None else acc[mp] + pv
                lsum[mp] = ls if lsum[mp] is None else lsum[mp] + ls
            pace = _exact_zero_like(ls[0:1, 0:128]).astype(_BF)
        for mp in range(2):
            mx_w[mp] = mrun[mp]
        l0 = jnp.sum(lsum[0], axis=0, keepdims=True)
        l1 = jnp.sum(lsum[1], axis=0, keepdims=True)
        o = acc[0] * (1.0 / l0) - acc[1] * (lam / l1)
        ms = jnp.mean(o * o, axis=0, keepdims=True)
        o = o * lax.rsqrt(ms + EPS) * (sub_ref[...] * (1.0 - lam_init))
        o_ref[0] = o.T.astype(o_ref.dtype)

    @pl.when(t % 2 == 0)
    def _():
        step(s0_ref, mx0_ref, s1_ref, mx1_ref)

    @pl.when(t % 2 == 1)
    def _():
        step(s1_ref, mx1_ref, s0_ref, mx0_ref)


def _diff_attention_ctx(q, ka, va, kb, vb, lam, subln, *, lam_init, tq, tk):
    b, lq, _ = q.shape
    la, lb = ka.shape[1], kb.shape[1]
    nq = lq // tq
    units = b * DA_HEADS * nq

    def unit_index(u):
        bh = u // nq
        return bh // DA_HEADS, bh % DA_HEADS, u % nq

    def score_unit(t):
        return unit_index(jnp.minimum(t, units - 1))

    def value_unit(t):
        return unit_index(jnp.maximum(t - 1, 0))

    def q_map(t):
        bi, hi, qi = score_unit(t)
        return bi, qi, hi

    def k_map(t):
        bi, hi, _ = score_unit(t)
        return bi, 0, hi

    def v_map(t):
        bi, hi, _ = value_unit(t)
        return bi, 0, hi

    def o_map(t):
        bi, hi, qi = value_unit(t)
        return bi, qi, hi

    score_buf = pltpu.VMEM((2, la + lb, tq), _F32)
    max_buf = pltpu.VMEM((2, 8, tq), _F32)
    return pl.pallas_call(
        functools.partial(_attn_ctx_kernel, nq=nq, tk=tk, lam_init=lam_init),
        grid=(units + 1,),
        in_specs=[pl.BlockSpec((1, tq, 128), q_map),
                  pl.BlockSpec((1, la, 128), k_map), pl.BlockSpec((1, lb, 128), k_map),
                  pl.BlockSpec((1, la, 128), v_map), pl.BlockSpec((1, lb, 128), v_map),
                  pl.BlockSpec((4, DA_DK), lambda t: (0, 0)), pl.BlockSpec((DA_DV, 1), lambda t: (0, 0))],
        out_specs=pl.BlockSpec((1, tq, 128), o_map),
        out_shape=jax.ShapeDtypeStruct((b, lq, D_MODEL), _BF),
        scratch_shapes=[pltpu.VMEM((128, la), _BF), pltpu.VMEM((lb // tk, 128, tk), _BF),
                        score_buf, score_buf, max_buf, max_buf],
        compiler_params=_params(("arbitrary",)),
        name="diff_attn_ctx",
    )(q, ka, kb, va, vb, lam, subln.reshape(DA_DV, 1))


def _halo_specs(rows, tm, halo):
    assert tm % halo == 0 and rows % tm == 0
    per = tm // halo
    nblk = rows // halo
    main = pl.BlockSpec((tm, D_MODEL), lambda i: (i, 0))
    prev = pl.BlockSpec((halo, D_MODEL), lambda i: (jnp.maximum(i * per - 1, 0), 0))
    nxt = pl.BlockSpec((halo, D_MODEL), lambda i: (jnp.minimum((i + 1) * per, nblk - 1), 0))
    return [prev, main, nxt]


def _seq_keep_flags(tm, seq_len, tile):
    tiles = seq_len // tm
    t = tile % tiles
    return jnp.where(t != 0, 1.0, 0.0), jnp.where(t != tiles - 1, 1.0, 0.0)


def _modulated_ext(prev, main, nxt, shift, scale, seq_len, tile):
    keep_prev, keep_next = _seq_keep_flags(main.shape[0], seq_len, tile)
    mod = lambda r: r * (1.0 + scale) + shift
    main = mod(main)
    ext = jnp.concatenate([mod(prev) * keep_prev, main, mod(nxt) * keep_next], axis=0)
    return ext.astype(_BF), main.astype(_BF)


def _conv3(u, w, b, tm):
    rows = u.shape[0]
    halo = (rows - tm) // 2
    up = pltpu.roll(u, 1, 0)[halo:halo + tm]
    un = pltpu.roll(u, rows - 1, 0)[halo:halo + tm]
    return up * w[0:1, :] + u[halo:halo + tm] * w[1:2, :] + un * w[2:3, :] + b


def _resident(shape):
    return pl.BlockSpec(shape, lambda i: (0,) * len(shape), pipeline_mode=pl.Buffered(1))


def _mix_ffn_kernel(*refs, seq_len, mlstm, halo):
    refs = list(refs)
    per = 3 if halo else 1
    n_act = 3 if mlstm else 1
    acts = [refs[per * i:per * i + per] for i in range(n_act)]
    x_refs = refs[per * n_act:per * n_act + per]
    rest = refs[per * n_act + per:]
    if mlstm:
        nw_ref, rest = rest[0], rest[1:]
    m_ref, wo_ref, g1_ref, b1_ref, wu_ref, cw_ref, cb_ref, wd_ref, g2_ref, b2_ref, o_ref, a_ref = rest
    tm = o_ref.shape[0]
    rows = lambda group: jnp.concatenate([r[...] for r in group], axis=0)

    if mlstm:
        hsum = rows(acts[0]).astype(_F32) + rows(acts[1]).astype(_F32)
        parts = []
        for hd in range(ML_HEADS):
            hh = hsum[:, hd * ML_DV:(hd + 1) * ML_DV]
            mu = jnp.mean(hh, axis=-1, keepdims=True)
            hc = hh - mu
            var = jnp.mean(hc * hc, axis=-1, keepdims=True)
            parts.append(hc * lax.rsqrt(var + EPS))
        hn = jnp.concatenate(parts, axis=-1) * nw_ref[...]
        act = (rows(acts[2]).astype(_F32) * hn).astype(_BF)
    else:
        act = rows(acts[0])
    z1 = ALPHA * rows(x_refs) + m_ref[0, 2:3, :] * _dot(act, wo_ref[...])
    x1 = _layer_norm_rows(z1, g1_ref[...], b1_ref[...])

    shift, scale = m_ref[0, 3:4, :], m_ref[0, 4:5, :]
    if halo:
        h, _ = _modulated_ext(x1[0:halo], x1[halo:halo + tm], x1[halo + tm:], shift, scale, seq_len,
                              pl.program_id(0))
    else:
        h = (x1 * (1.0 + scale) + shift).astype(_BF)
    zero_rows = jnp.zeros((_HALO, _FF_CHUNK), _F32)

    def pre_conv(cols):
        u = _dot(h, wu_ref[:, cols])
        return u if halo else jnp.concatenate([zero_rows, u, zero_rows], axis=0)

    for j in range(D_FF // _FF_CHUNK):
        gcols = slice(j * _FF_CHUNK, (j + 1) * _FF_CHUNK)
        vcols = slice(D_FF + j * _FF_CHUNK, D_FF + (j + 1) * _FF_CHUNK)
        gate = _conv3(pre_conv(gcols), cw_ref[:, gcols], cb_ref[:, gcols], tm)
        val = _conv3(pre_conv(vcols), cw_ref[:, vcols], cb_ref[:, vcols], tm)
        a_ref[:, gcols] = (_silu(gate) * val).astype(_BF)
    z2 = ALPHA * x1[halo:halo + tm] + m_ref[0, 5:6, :] * _dot(a_ref[...], wd_ref[...])
    o_ref[...] = _layer_norm_rows(z2, g2_ref[...], b2_ref[...])


def _mix_ffn(acts, w_o_bf, x2d, mods, ln_g, ln_b, w_up_bf, conv_w, conv_b, w_down_bf, *,
             seq_len, seq_base, tm, norm_w=None):
    rows = x2d.shape[0]
    mlstm = norm_w is not None
    halo = 0 if tm == seq_len else _MIX_HALO
    tps = seq_len // tm if seq_base else None
    vec_spec = pl.BlockSpec((1, D_MODEL), lambda i: (0, 0))
    vec = lambda a: a.reshape(1, D_MODEL)
    in_specs, args = [], []
    for arr in list(acts) + [x2d]:
        if halo:
            in_specs += _halo_specs(rows, tm, halo)
            args += [arr, arr, arr]
        else:
            in_specs.append(pl.BlockSpec((tm, D_MODEL), lambda i: (i, 0)))
            args.append(arr)
    if mlstm:
        in_specs.append(vec_spec)
        args.append(vec(norm_w))
    in_specs += [_mod_spec(seq_base, tps), _resident((D_MODEL, D_MODEL)), vec_spec, vec_spec,
                 _resident((D_MODEL, 2 * D_FF)), _resident((3, 2 * D_FF)), _resident((1, 2 * D_FF)),
                 _resident((D_FF, D_MODEL)), vec_spec, vec_spec]
    args += [mods, w_o_bf, vec(ln_g[0]), vec(ln_b[0]), w_up_bf, conv_w, conv_b.reshape(1, 2 * D_FF), w_down_bf,
             vec(ln_g[1]), vec(ln_b[1])]
    return pl.pallas_call(
        functools.partial(_mix_ffn_kernel, seq_len=seq_len, mlstm=mlstm, halo=halo),
        grid=(rows // tm,),
        in_specs=in_specs,
        out_specs=pl.BlockSpec((tm, D_MODEL), lambda i: (i, 0)),
        out_shape=jax.ShapeDtypeStruct((rows, D_MODEL), _F32),
        scratch_shapes=[pltpu.VMEM((tm, D_FF), _BF)],
        compiler_params=_params(("arbitrary",)),
        name="ml_out_ffn" if mlstm else "da_out_ffn",
    )(*args)


def _mlproj_kernel(prev_ref, x_ref, next_ref, m_ref, w_ref, wgt_ref, cw_ref, cb_ref, bgt_ref,
                   q_ref, k_ref, v_ref, og_ref, gtt_ref, *, seq_len):
    tm = x_ref.shape[0]
    shift, scale = m_ref[0, 0:1, :], m_ref[0, 1:2, :]
    h, hm = _modulated_ext(prev_ref[...], x_ref[...], next_ref[...], shift, scale, seq_len, pl.program_id(0))
    ch = _FF_CHUNK
    v0, o0 = 2 * ML_QK, 2 * ML_QK + ML_V
    pace = None
    for i in range(2 * ML_QK // ch):
        lo = i * ch
        cols = slice(lo, lo + ch)
        u = _dot(h, w_ref[:, cols])
        if pace is not None:
            u = u + pace
        y = _silu(_conv3(u, cw_ref[:, cols], cb_ref[:, cols], tm))
        if lo < ML_QK:
            q_ref[:, cols] = y.astype(_BF)
        else:
            k_ref[:, lo - ML_QK:lo - ML_QK + ch] = (y * (ML_DK ** -0.5)).astype(_BF)
        vc = _dot(hm, w_ref[:, v0 + lo:v0 + lo + ch])
        oc = _dot(hm, w_ref[:, o0 + lo:o0 + lo + ch])
        v_ref[:, cols] = vc.astype(_BF)
        og_ref[:, cols] = _sigmoid(oc).astype(_BF)
        pace = _exact_zero_like(vc[0:1, :]) + _exact_zero_like(oc[0:1, :])
    wg_hi, wg_lo = _split2(wgt_ref[...])
    gtt_ref[...] = _dot_nt(wg_hi, hm) + _dot_nt(wg_lo, hm) + bgt_ref[...]


def _ml_proj(x2d, mods, w_main_bf, w_gate, conv_w, conv_b, b_gate, *, seq_len, seq_base, tm):
    rows = x2d.shape[0]
    tps = seq_len // tm if seq_base else None
    prev, main, nxt = _halo_specs(rows, tm, _HALO)
    n_main = 2 * ML_QK + 2 * ML_V
    full = lambda shape: pl.BlockSpec(shape, lambda i: (0,) * len(shape))
    row = lambda n: pl.BlockSpec((tm, n), lambda i: (i, 0))
    return pl.pallas_call(
        functools.partial(_mlproj_kernel, seq_len=seq_len),
        grid=(rows // tm,),
        in_specs=[prev, main, nxt, _mod_spec(seq_base, tps), full((D_MODEL, n_main)),
                  full((ML_GATES, D_MODEL)), full((3, 2 * ML_QK)), full((1, 2 * ML_QK)), full((ML_GATES, 1))],
        out_specs=[row(ML_QK), row(ML_QK), row(ML_V), row(ML_V),
                   pl.BlockSpec((ML_GATES, tm), lambda i: (0, i))],
        out_shape=[jax.ShapeDtypeStruct((rows, ML_QK), _BF), jax.ShapeDtypeStruct((rows, ML_QK), _BF),
                   jax.ShapeDtypeStruct((rows, ML_V), _BF), jax.ShapeDtypeStruct((rows, ML_V), _BF),
                   jax.ShapeDtypeStruct((ML_GATES, rows), _F32)],
        compiler_params=_params(("arbitrary",)),
        name="ml_inproj_conv",
    )(x2d, x2d, x2d, mods, w_main_bf, w_gate.T, conv_w, conv_b.reshape(1, 2 * ML_QK),
      b_gate.reshape(ML_GATES, 1))


def _mlstm_chunk(q, k, v, i_row, b_col, b_row, total, m_prev, c_prev, n_prev, causal):
    t = q.shape[0]
    wide = lambda col: jnp.broadcast_to(col, (t, 128))
    tile2 = lambda w: jnp.concatenate([w] * (t // 128), axis=1)
    ti = lax.broadcasted_iota(jnp.int32, (t, t), 0)
    si = lax.broadcasted_iota(jnp.int32, (t, t), 1)
    a_row = i_row - b_row
    amat = jnp.where((si <= ti) if causal else (si >= ti), a_row, _NEG)
    g_col = jnp.maximum(m_prev, jnp.max(amat, axis=1, keepdims=True))
    g_w = wide(g_col)
    w_intra = jnp.exp(amat - tile2(g_w))
    w_inter_w = jnp.exp(m_prev - g_w)
    s = _dot_nt(q, k) * w_intra
    den_parts = w_inter_w * (q.astype(_F32) * n_prev)
    for c in range(t // 128):
        den_parts = den_parts + s[:, c * 128:(c + 1) * 128]
    den_col = jnp.sum(den_parts, axis=1, keepdims=True)
    r_col = 1.0 / jnp.maximum(jnp.abs(den_col), jnp.exp(-(b_col + g_col)))
    num = jnp.concatenate([w_inter_w, w_inter_w], axis=1) * _dot(q, c_prev.astype(_BF)) + _dot(s.astype(_BF), v)
    r_w = wide(r_col)
    h = num * jnp.concatenate([r_w, r_w], axis=1)
    g_end = jnp.maximum(m_prev, jnp.max(a_row, axis=1, keepdims=True))
    w_s = jnp.exp(a_row - g_end)
    carry = jnp.exp(m_prev - g_end)
    kw_t = (k.astype(_F32).T * w_s).astype(_BF)
    c_new = carry * c_prev + _dot(kw_t, v)
    ws_hi, ws_lo = _split2(jnp.broadcast_to(w_s, (8, t)))
    n_new = carry * n_prev + (_dot(ws_hi, k) + _dot(ws_lo, k))[0:1, :]
    return h, c_new, n_new, total + g_end


def _mlstm_kernel(*refs, has_init, want_state):
    refs = list(refs)
    (qf_ref, kf_ref, vf_ref, gtf_ref, qb_ref, kb_ref, vb_ref, gtb_ref) = refs[:8]
    pos = 8
    if has_init:
        c0_ref, n0_ref, m0_ref = refs[pos:pos + 3]
        pos += 3
    hf_ref, hb_ref = refs[pos:pos + 2]
    pos += 2
    if want_state:
        co_ref, no_ref, mo_ref = refs[pos:pos + 3]
        pos += 3
    c_ref, n_ref, m_ref = refs[pos:pos + 3]
    ci = pl.program_id(1)
    t = qf_ref.shape[1]

    @pl.when(ci == 0)
    def _():
        if has_init:
            c_ref[...] = c0_ref[0]
            n_ref[...] = n0_ref[0]
            m_ref[...] = jnp.broadcast_to(m0_ref[0], m_ref.shape)
        else:
            c_ref[...] = jnp.zeros(c_ref.shape, _F32)
            n_ref[...] = jnp.zeros(n_ref.shape, _F32)
            m_ref[...] = jnp.zeros(m_ref.shape, _F32)

    ri = lax.broadcasted_iota(jnp.int32, (t, t), 0)
    cj = lax.broadcasted_iota(jnp.int32, (t, t), 1)
    lower = (cj <= ri).astype(_BF)
    upper = (cj >= ri).astype(_BF)

    def cum(mat, xt):
        return sum(_dot_nt(mat, p) for p in _split3(xt))

    def cum_t(xt, mat):
        return sum(_dot(p, mat) for p in _split3(xt))

    m_all, n_all = m_ref[...], n_ref[...]
    c_all = [c_ref[st] for st in range(2 * ML_HEADS)]
    results = []
    for d, (q_ref, k_ref, v_ref, gt_ref, h_ref) in enumerate(
            ((qf_ref, kf_ref, vf_ref, gtf_ref, hf_ref), (qb_ref, kb_ref, vb_ref, gtb_ref, hb_ref))):
        gt = gt_ref[...]
        is_f_row = (lax.broadcasted_iota(jnp.int32, (ML_GATES, 1), 0) % 8) >= 4
        xt = jnp.where(is_f_row, _log_sigmoid(gt), gt)
        if d == 0:
            bc = cum(lower, xt)
            br = cum_t(xt, upper)
        else:
            bc = cum(upper, xt)
            br = cum_t(xt, lower)
        tot = jnp.sum(xt, axis=1, keepdims=True)
        for hd in range(ML_HEADS):
            ic, fc = d * 8 + hd, d * 8 + 4 + hd
            st = d * ML_HEADS + hd
            h, c_new, n_new, m_new = _mlstm_chunk(
                q_ref[0, :, hd * ML_DK:(hd + 1) * ML_DK], k_ref[0, :, hd * ML_DK:(hd + 1) * ML_DK],
                v_ref[0, :, hd * ML_DV:(hd + 1) * ML_DV],
                xt[ic:ic + 1, :], bc[:, fc:fc + 1], br[fc:fc + 1, :],
                tot[fc:fc + 1, :], m_all[st:st + 1, 0:1], c_all[st], n_all[st:st + 1, :], d == 0)
            results.append((h_ref, hd, st, h, c_new, n_new, m_new))
    for h_ref, hd, st, h, c_new, n_new, m_new in results:
        h_ref[0, :, hd * ML_DV:(hd + 1) * ML_DV] = h.astype(h_ref.dtype)
        c_ref[st] = c_new
        n_ref[st:st + 1, :] = n_new
        m_ref[st:st + 1, :] = jnp.broadcast_to(m_new, (1, 128))

    if want_state:
        @pl.when(ci == pl.num_programs(1) - 1)
        def _():
            co_ref[0] = c_ref[...]
            no_ref[0] = n_ref[...]
            mo_ref[0] = m_ref[...]


def _mlstm_scan(q, k, v, gates_t, init, *, want_state):
    b, l, _ = q.shape
    t = _ML_CHUNK
    nc = l // t
    fwd = lambda n: pl.BlockSpec((1, t, n), lambda bi, ci: (bi, ci, 0))
    bwd = lambda n: pl.BlockSpec((1, t, n), lambda bi, ci: (bi, nc - 1 - ci, 0))
    gtf = pl.BlockSpec((ML_GATES, t), lambda bi, ci: (0, bi * nc + ci))
    gtb = pl.BlockSpec((ML_GATES, t), lambda bi, ci: (0, bi * nc + nc - 1 - ci))
    in_specs = [fwd(ML_QK), fwd(ML_QK), fwd(ML_V), gtf, bwd(ML_QK), bwd(ML_QK), bwd(ML_V), gtb]
    args = [q, k, v, gates_t, q, k, v, gates_t]
    has_init = init is not None
    if has_init:
        c0, n0, m0 = init
        in_specs += [pl.BlockSpec((1, 8, ML_DK, ML_DV), lambda bi, ci: (bi, 0, 0, 0)),
                     pl.BlockSpec((1, 8, ML_DK), lambda bi, ci: (bi, 0, 0)),
                     pl.BlockSpec((1, 8, 1), lambda bi, ci: (bi, 0, 0))]
        args += [c0.reshape(b, 8, ML_DK, ML_DV), n0.reshape(b, 8, ML_DK), m0.reshape(b, 8, 1)]
    out_specs = [fwd(ML_V), bwd(ML_V)]
    out_shape = [jax.ShapeDtypeStruct((b, l, ML_V), _BF), jax.ShapeDtypeStruct((b, l, ML_V), _BF)]
    if want_state:
        out_specs += [pl.BlockSpec((1, 8, ML_DK, ML_DV), lambda bi, ci: (bi, 0, 0, 0)),
                      pl.BlockSpec((1, 8, ML_DK), lambda bi, ci: (bi, 0, 0)),
                      pl.BlockSpec((1, 8, 128), lambda bi, ci: (bi, 0, 0))]
        out_shape += [jax.ShapeDtypeStruct((b, 8, ML_DK, ML_DV), _F32),
                      jax.ShapeDtypeStruct((b, 8, ML_DK), _F32),
                      jax.ShapeDtypeStruct((b, 8, 128), _F32)]
    return pl.pallas_call(
        functools.partial(_mlstm_kernel, has_init=has_init, want_state=want_state),
        grid=(b, nc),
        in_specs=in_specs,
        out_specs=out_specs,
        out_shape=out_shape,
        scratch_shapes=[pltpu.VMEM((8, ML_DK, ML_DV), _F32), pltpu.VMEM((8, ML_DK), _F32),
                        pltpu.VMEM((8, 128), _F32)],
        compiler_params=_params(("arbitrary", "arbitrary")),
        name="mlstm_scan_state" if want_state else "mlstm_scan",
    )(*args)


def kernel(x_prompt, x_sample, c, cache_k, cache_v, state_C, state_n, state_m, c_ctx, ada_w, ada_b, ln_g, ln_b,
           da_w_qkv, da_lam, da_subln, da_w_o, ml_w_in, ml_conv_w, ml_conv_b, ml_b_gate, ml_norm_w, ml_w_out,
           ffn_w_up, ffn_conv_w, ffn_conv_b, ffn_w_down):
    bp, lp, _ = x_prompt.shape
    bs, ls, _ = x_sample.shape
    cond = jnp.concatenate([c_ctx[None, :], c, jnp.zeros((8 - 1 - bs, D_MODEL), _F32)], axis=0)
    mods = _modulation(cond, ada_w, ada_b).reshape(DEPTH, 8, 6, D_MODEL)

    xp = x_prompt.reshape(bp * lp, D_MODEL)
    xs = x_sample.reshape(bs * ls, D_MODEL)
    groups = (dict(seq_len=lp, seq_base=0), dict(seq_len=ls, seq_base=1))

    lam_init = 0.8 - 0.6 * math.exp(-0.3 * 0)
    w_qkv = da_w_qkv[0].astype(_BF)
    w_o = da_w_o[0].astype(_BF)
    qp, kp, vp = _qkv_proj(xp, mods[0], w_qkv, rope=False, kv_dtype=_F32, tm=_ROW_TILE, **groups[0])
    qs, ks, vs = _qkv_proj(xs, mods[0], w_qkv, rope=True, kv_dtype=_BF, tm=_ROW_TILE, **groups[1])
    as3 = lambda a, b: a.reshape(b, -1, D_MODEL)
    op = _diff_attention(as3(qp, bp), as3(kp, bp), as3(vp, bp), None, None, da_lam[0], da_subln[0],
                         lam_init=lam_init, tq=256, tk=512, heads=DA_HEADS)
    os_ = _diff_attention_ctx(as3(qs, bs), cache_k[:, 0].reshape(bs, -1, D_MODEL),
                              cache_v[:, 0].reshape(bs, -1, D_MODEL), as3(ks, bs), as3(vs, bs),
                              da_lam[0], da_subln[0], lam_init=lam_init, tq=256, tk=512)
    new_k = kp.reshape(bp, 1, lp, DA_HEADS, 2, DA_DK)
    new_v = vp.reshape(bp, 1, lp, DA_HEADS, DA_DV)

    def mix_ffn(acts, w_mix, x2d, i, grp, **kw):
        return _mix_ffn(acts, w_mix, x2d, mods[i], ln_g[i], ln_b[i], ffn_w_up[i].astype(_BF), ffn_conv_w[i],
                        ffn_conv_b[i], ffn_w_down[i].astype(_BF), tm=min(_ROW_TILE, grp["seq_len"]), **grp, **kw)

    xp = mix_ffn([op.reshape(-1, D_MODEL)], w_o, xp, 0, groups[0])
    xs = mix_ffn([os_.reshape(-1, D_MODEL)], w_o, xs, 0, groups[1])

    n_main = 2 * ML_QK + 2 * ML_V
    w_main = ml_w_in[0][:, :n_main].astype(_BF)
    w_gate = ml_w_in[0][:, n_main:]
    w_out = ml_w_out[0].astype(_BF)
    outs = []
    for x2d, grp, nb, init in ((xp, groups[0], bp, None),
                               (xs, groups[1], bs, (state_C[:, 0], state_n[:, 0], state_m[:, 0]))):
        q, k, v, og, gtt = _ml_proj(x2d, mods[1], w_main, w_gate, ml_conv_w[0], ml_conv_b[0], ml_b_gate[0],
                                        tm=min(_ROW_TILE, grp["seq_len"]), **grp)
        r3 = lambda a: a.reshape(nb, -1, a.shape[-1])
        res = _mlstm_scan(r3(q), r3(k), r3(v), gtt, init, want_state=init is None)
        hf, hb = res[0].reshape(-1, ML_V), res[1].reshape(-1, ML_V)
        outs.append((mix_ffn([hf, hb, og], w_out, x2d, 1, grp, norm_w=ml_norm_w[0]), res[2:]))
    (xp, (c_fin, n_fin, m_fin)), (xs, _) = outs
    new_c = c_fin.reshape(bp, 1, 2, ML_HEADS, ML_DK, ML_DV)
    new_n = n_fin.reshape(bp, 1, 2, ML_HEADS, ML_DK)
    new_m = m_fin[:, :, 0].reshape(bp, 1, 2, ML_HEADS)
    return (xp.reshape(bp, lp, D_MODEL), xs.reshape(bs, ls, D_MODEL), new_k, new_v, new_c, new_n, new_m)
```

```python
import functools
import math

import jax
import jax.numpy as jnp
from jax import lax
from jax.experimental import pallas as pl
from jax.experimental.pallas import tpu as pltpu

D_MODEL = 1024
DEPTH = 2
GRID_W = 64
DA_HEADS = 8
DA_DK = 64
DA_DV = 128
ML_HEADS = 4
ML_DK = 128
ML_DV = 256
ML_QK = ML_HEADS * ML_DK
ML_V = ML_HEADS * ML_DV
ML_GATES = 16
D_FF = 2816
ROPE_THETA = 10000.0
ALPHA = (2 * DEPTH) ** 0.25
EPS = 1e-5

_BF = jnp.bfloat16
_F32 = jnp.float32
_NEG = -1e30

_VMEM_LIMIT_BYTES = 56 * 1024 * 1024
_HALO = 8
_MIX_HALO = 16
_ML_CHUNK = 256
_ROW_TILE = 512
_FF_CHUNK = 256
_Q_SCALE = DA_DK ** -0.5 * math.log2(math.e)


def _dot(a, b):
    return jnp.dot(a, b, preferred_element_type=_F32)


def _dot_nt(a, b):
    return lax.dot_general(a, b, (((1,), (1,)), ((), ())), preferred_element_type=_F32)


def _split2(x):
    hi = x.astype(_BF)
    lo = (x - hi.astype(_F32)).astype(_BF)
    return hi, lo


def _split3(x):
    hi = x.astype(_BF)
    r = x - hi.astype(_F32)
    mid = r.astype(_BF)
    lo = (r - mid.astype(_F32)).astype(_BF)
    return hi, mid, lo


def _dot_f32(a, b):
    ah, al = _split2(a)
    bh, bl = _split2(b)
    return _dot(ah, bh) + _dot(al, bh) + _dot(ah, bl)


def _sigmoid(x):
    return 1.0 / (1.0 + jnp.exp(-x))


def _silu(x):
    return x * _sigmoid(x)


def _log_sigmoid(x):
    return jnp.minimum(x, 0.0) - jnp.log(1.0 + jnp.exp(-jnp.abs(x)))


def _layer_norm_rows(z, g, b):
    mu = jnp.mean(z, axis=-1, keepdims=True)
    zc = z - mu
    var = jnp.mean(zc * zc, axis=-1, keepdims=True)
    return zc * lax.rsqrt(var + EPS) * g + b


def _params(sem):
    return pltpu.CompilerParams(dimension_semantics=sem, vmem_limit_bytes=_VMEM_LIMIT_BYTES)


def _mod_kernel(c_ref, w_ref, b_ref, o_ref):
    s = _silu(c_ref[...])
    o_ref[0] = _dot_f32(s, w_ref[0]) + b_ref[0]


def _modulation(cond, ada_w, ada_b):
    tn = 1024
    n = 6 * D_MODEL
    return pl.pallas_call(
        _mod_kernel,
        grid=(DEPTH, n // tn),
        in_specs=[
            pl.BlockSpec((8, D_MODEL), lambda l, j: (0, 0)),
            pl.BlockSpec((1, D_MODEL, tn), lambda l, j: (l, 0, j)),
            pl.BlockSpec((1, 1, tn), lambda l, j: (l, 0, j)),
        ],
        out_specs=pl.BlockSpec((1, 8, tn), lambda l, j: (l, 0, j)),
        out_shape=jax.ShapeDtypeStruct((DEPTH, 8, n), _F32),
        compiler_params=_params(("arbitrary", "arbitrary")),
        name="adaln_mod",
    )(cond, ada_w, ada_b.reshape(DEPTH, 1, n))


def _mod_spec(seq_base, tiles_per_seq):
    if tiles_per_seq is None:
        return pl.BlockSpec((1, 6, D_MODEL), lambda i, *_: (seq_base, 0, 0))
    return pl.BlockSpec((1, 6, D_MODEL), lambda i, *_: (seq_base + i // tiles_per_seq, 0, 0))


def _qkv_kernel(*refs, rope):
    if rope:
        x_ref, m_ref, w_ref, cos_ref, sa_ref, sb_ref, q_ref, k_ref, v_ref = refs
    else:
        x_ref, m_ref, w_ref, q_ref, k_ref, v_ref = refs
    h = (x_ref[...] * (1.0 + m_ref[0, 1:2, :]) + m_ref[0, 0:1, :]).astype(_BF)
    for c, o_ref in enumerate((q_ref, k_ref, v_ref)):
        y = _dot(h, w_ref[:, c * D_MODEL:(c + 1) * D_MODEL])
        if rope and c < 2:
            cos, sa, sb = cos_ref[...], sa_ref[...], sb_ref[...]
            for hd in range(DA_HEADS):
                yh = y[:, hd * 128:(hd + 1) * 128]
                yh = yh * cos + pltpu.roll(yh, 112, 1) * sa + pltpu.roll(yh, 16, 1) * sb
                if c == 0:
                    yh = yh * _Q_SCALE
                o_ref[:, hd * 128:(hd + 1) * 128] = yh.astype(o_ref.dtype)
        else:
            if c == 0:
                y = y * _Q_SCALE
            o_ref[...] = y.astype(o_ref.dtype)


def _rope_tables(seq_len):
    t = jnp.arange(seq_len)
    row = (t // GRID_W).astype(_F32)
    col = (t % GRID_W).astype(_F32)
    lane = jnp.arange(128)
    d = lane % DA_DK
    half = DA_DK // 2
    nf = half // 2
    dd = d % half
    f = dd % nf
    odd = (dd // nf) == 1
    inv = ROPE_THETA ** (-jnp.arange(nf, dtype=_F32) / nf)
    pos = jnp.where((d < half)[None, :], row[:, None], col[:, None])
    ang = pos * inv[f][None, :]
    cos, sin = jnp.cos(ang), jnp.sin(ang)
    sa = jnp.where(odd[None, :], 0.0, -sin)
    sb = jnp.where(odd[None, :], sin, 0.0)
    return cos, sa, sb


def _qkv_proj(x2d, mods, w_bf, *, seq_len, seq_base, rope, kv_dtype, tm):
    rows = x2d.shape[0]
    tps = seq_len // tm if seq_base else None
    in_specs = [
        pl.BlockSpec((tm, D_MODEL), lambda i: (i, 0)),
        _mod_spec(seq_base, tps),
        pl.BlockSpec((D_MODEL, 3 * D_MODEL), lambda i: (0, 0)),
    ]
    args = [x2d, mods, w_bf]
    if rope:
        tabs = _rope_tables(seq_len)
        in_specs += [pl.BlockSpec((tm, 128), lambda i: (i % (seq_len // tm), 0))] * 3
        args += list(tabs)
    out_spec = pl.BlockSpec((tm, D_MODEL), lambda i: (i, 0))
    return pl.pallas_call(
        functools.partial(_qkv_kernel, rope=rope),
        grid=(rows // tm,),
        in_specs=in_specs,
        out_specs=[out_spec, out_spec, out_spec],
        out_shape=[jax.ShapeDtypeStruct((rows, D_MODEL), _BF),
                   jax.ShapeDtypeStruct((rows, D_MODEL), kv_dtype),
                   jax.ShapeDtypeStruct((rows, D_MODEL), kv_dtype)],
        compiler_params=_params(("arbitrary",)),
        name="da_qkv_rope" if rope else "da_qkv",
    )(*args)


_ATTN_TQ = 256
_ATTN_TK = 512


def _group_rows(x):
    return x.reshape(x.shape[0] // 8, 8, x.shape[1])


def _diff_lambda(lam_ref, lam_init):
    lf = lam_ref[...]
    return (jnp.exp(jnp.sum(lf[0:1] * lf[1:2], axis=1, keepdims=True))
            - jnp.exp(jnp.sum(lf[2:3] * lf[3:4], axis=1, keepdims=True)) + lam_init)


def _split_maps(q):
    first_map = lax.broadcasted_iota(jnp.int32, q.shape, 1) < DA_DK
    return jnp.where(first_map, q, jnp.zeros_like(q)), jnp.where(first_map, jnp.zeros_like(q), q)


def _diff_output(acc, lsum, lam, sub_ref, lam_init):
    l0 = jnp.sum(lsum[0], axis=0, keepdims=True)
    l1 = jnp.sum(lsum[1], axis=0, keepdims=True)
    o = acc[0] * (1.0 / l0) - acc[1] * (lam / l1)
    ms = jnp.mean(o * o, axis=0, keepdims=True)
    o = o * lax.rsqrt(ms + EPS) * (sub_ref[...] * (1.0 - lam_init))
    return o.T


def _attn_self_kernel(q_ref, k_ref, v_ref, lam_ref, sub_ref, o_ref, s_ref, *, lam_init):
    lam = _diff_lambda(lam_ref, lam_init)
    lanes = lambda hd: slice(hd * 128, (hd + 1) * 128)
    mx = []
    for hd in range(DA_HEADS):
        qm = _split_maps(q_ref[0, :, lanes(hd)])
        k = k_ref[0, :, lanes(hd)].astype(_BF)
        for mp in range(2):
            s = _dot_nt(k, qm[mp])
            s_ref[hd, mp] = s
            mx.append(jnp.max(jnp.max(_group_rows(s), axis=0), axis=0, keepdims=True))
    for hd in range(DA_HEADS):
        vt = v_ref[0, :, lanes(hd)].T.astype(_BF)
        acc, lsum = [], []
        for mp in range(2):
            e = jnp.exp2(s_ref[hd, mp] - mx[2 * hd + mp])
            acc.append(_dot(vt, e.astype(_BF)))
            lsum.append(jnp.sum(_group_rows(e), axis=0))
        o_ref[0, :, lanes(hd)] = _diff_output(acc, lsum, lam, sub_ref, lam_init).astype(o_ref.dtype)


def _self_attention(q, k, v, lam, subln, *, lam_init):
    b, l, _ = q.shape
    seq = pl.BlockSpec((1, l, D_MODEL), lambda bi: (bi, 0, 0))
    return pl.pallas_call(
        functools.partial(_attn_self_kernel, lam_init=lam_init),
        grid=(b,),
        in_specs=[seq, seq, seq, pl.BlockSpec((4, DA_DK), lambda bi: (0, 0)),
                  pl.BlockSpec((DA_DV, 1), lambda bi: (0, 0))],
        out_specs=seq,
        out_shape=jax.ShapeDtypeStruct((b, l, D_MODEL), _BF),
        scratch_shapes=[pltpu.VMEM((DA_HEADS, 2, l, l), _F32)],
        compiler_params=_params(("arbitrary",)),
        name="diff_attn",
    )(q, k, v, lam, subln.reshape(DA_DV, 1))


def _exact_zero_like(x):
    u = lax.bitcast_convert_type(x, jnp.uint32)
    u = lax.shift_right_logical(lax.shift_right_logical(u, jnp.uint32(16)), jnp.uint32(16))
    return lax.bitcast_convert_type(u, _F32)


def _attn_ctx_kernel(q_ref, ka_ref, kb_ref, va_ref, vb_ref, lam_ref, sub_ref, o_ref,
                     vta_ref, vtb_ref, s0_ref, s1_ref, mx0_ref, mx1_ref, *, nq, tk, lam_init):
    t = pl.program_id(0)
    la = ka_ref.shape[1]
    n_new = kb_ref.shape[1] // tk

    @pl.when(t == 0)
    def _():
        s1_ref[...] = jnp.zeros(s1_ref.shape, _F32)
        mx1_ref[...] = jnp.zeros(mx1_ref.shape, _F32)

    @pl.when(jnp.maximum(t - 1, 0) % nq == 0)
    def _():
        vta_ref[...] = va_ref[0].astype(_F32).T.astype(_BF)
        for j in range(n_new):
            vtb_ref[j] = vb_ref[0, j * tk:(j + 1) * tk, :].astype(_F32).T.astype(_BF)

    lam = _diff_lambda(lam_ref, lam_init)

    tiles = [(0, la, lambda: ka_ref[0].astype(_BF), lambda: vta_ref[...])]
    for j in range(n_new):
        tiles.append((la + j * tk, tk, lambda j=j: kb_ref[0, j * tk:(j + 1) * tk, :], lambda j=j: vtb_ref[j]))

    def step(s_w, mx_w, s_r, mx_r):
        qm = _split_maps(q_ref[0])
        mx = [jnp.max(mx_r[mp], axis=0, keepdims=True) for mp in range(2)]
        mrun, acc, lsum = [None, None], [None, None], [None, None]
        pace = None
        for off, rows, load_k, load_vt in tiles:
            kt = load_k()
            if pace is not None:
                kt = kt + pace
            for mp in range(2):
                s = _dot_nt(kt, qm[mp])
                s_w[mp, off:off + rows, :] = s
                m = jnp.max(_group_rows(s), axis=0)
                mrun[mp] = m if mrun[mp] is None else jnp.maximum(mrun[mp], m)
            for mp in range(2):
                e = jnp.exp2(s_r[mp, off:off + rows, :] - mx[mp])
                pv = _dot(load_vt(), e.astype(_BF))
                ls = jnp.sum(_group_rows(e), axis=0)
                acc[mp] = pv if acc[mp] is None else acc[mp] + pv
                lsum[mp] = ls if lsum[mp] is None else lsum[mp] + ls
            pace = _exact_zero_like(ls[0:1, 0:128]).astype(_BF)
        for mp in range(2):
            mx_w[mp] = mrun[mp]
        o_ref[0] = _diff_output(acc, lsum, lam, sub_ref, lam_init).astype(o_ref.dtype)

    @pl.when(t % 2 == 0)
    def _():
        step(s0_ref, mx0_ref, s1_ref, mx1_ref)

    @pl.when(t % 2 == 1)
    def _():
        step(s1_ref, mx1_ref, s0_ref, mx0_ref)


def _diff_attention_ctx(q, ka, va, kb, vb, lam, subln, *, lam_init, tq, tk):
    b, lq, _ = q.shape
    la, lb = ka.shape[1], kb.shape[1]
    nq = lq // tq
    units = b * DA_HEADS * nq

    def unit_index(u):
        bh = u // nq
        return bh // DA_HEADS, bh % DA_HEADS, u % nq

    def score_unit(t):
        return unit_index(jnp.minimum(t, units - 1))

    def value_unit(t):
        return unit_index(jnp.maximum(t - 1, 0))

    def q_map(t):
        bi, hi, qi = score_unit(t)
        return bi, qi, hi

    def k_map(t):
        bi, hi, _ = score_unit(t)
        return bi, 0, hi

    def v_map(t):
        bi, hi, _ = value_unit(t)
        return bi, 0, hi

    def o_map(t):
        bi, hi, qi = value_unit(t)
        return bi, qi, hi

    score_buf = pltpu.VMEM((2, la + lb, tq), _F32)
    max_buf = pltpu.VMEM((2, 8, tq), _F32)
    return pl.pallas_call(
        functools.partial(_attn_ctx_kernel, nq=nq, tk=tk, lam_init=lam_init),
        grid=(units + 1,),
        in_specs=[pl.BlockSpec((1, tq, 128), q_map),
                  pl.BlockSpec((1, la, 128), k_map), pl.BlockSpec((1, lb, 128), k_map),
                  pl.BlockSpec((1, la, 128), v_map), pl.BlockSpec((1, lb, 128), v_map),
                  pl.BlockSpec((4, DA_DK), lambda t: (0, 0)), pl.BlockSpec((DA_DV, 1), lambda t: (0, 0))],
        out_specs=pl.BlockSpec((1, tq, 128), o_map),
        out_shape=jax.ShapeDtypeStruct((b, lq, D_MODEL), _BF),
        scratch_shapes=[pltpu.VMEM((128, la), _BF), pltpu.VMEM((lb // tk, 128, tk), _BF),
                        score_buf, score_buf, max_buf, max_buf],
        compiler_params=_params(("arbitrary",)),
        name="diff_attn_ctx",
    )(q, ka, kb, va, vb, lam, subln.reshape(DA_DV, 1))


def _halo_specs(rows, tm, halo):
    assert tm % halo == 0 and rows % tm == 0
    per = tm // halo
    nblk = rows // halo
    main = pl.BlockSpec((tm, D_MODEL), lambda i: (i, 0))
    prev = pl.BlockSpec((halo, D_MODEL), lambda i: (jnp.maximum(i * per - 1, 0), 0))
    nxt = pl.BlockSpec((halo, D_MODEL), lambda i: (jnp.minimum((i + 1) * per, nblk - 1), 0))
    return [prev, main, nxt]


def _seq_keep_flags(tm, seq_len, tile):
    tiles = seq_len // tm
    t = tile % tiles
    return jnp.where(t != 0, 1.0, 0.0), jnp.where(t != tiles - 1, 1.0, 0.0)


def _modulated_ext(prev, main, nxt, shift, scale, seq_len, tile):
    keep_prev, keep_next = _seq_keep_flags(main.shape[0], seq_len, tile)
    mod = lambda r: r * (1.0 + scale) + shift
    main = mod(main)
    ext = jnp.concatenate([mod(prev) * keep_prev, main, mod(nxt) * keep_next], axis=0)
    return ext.astype(_BF), main.astype(_BF)


def _conv3(u, w, b, tm):
    rows = u.shape[0]
    halo = (rows - tm) // 2
    up = pltpu.roll(u, 1, 0)[halo:halo + tm]
    un = pltpu.roll(u, rows - 1, 0)[halo:halo + tm]
    return up * w[0:1, :] + u[halo:halo + tm] * w[1:2, :] + un * w[2:3, :] + b


def _resident(shape):
    return pl.BlockSpec(shape, lambda i: (0,) * len(shape), pipeline_mode=pl.Buffered(1))


def _mix_ffn_kernel(*refs, seq_len, mlstm, halo):
    refs = list(refs)
    per = 3 if halo else 1
    n_act = 3 if mlstm else 1
    acts = [refs[per * i:per * i + per] for i in range(n_act)]
    x_refs = refs[per * n_act:per * n_act + per]
    rest = refs[per * n_act + per:]
    if mlstm:
        nw_ref, rest = rest[0], rest[1:]
    m_ref, wo_ref, g1_ref, b1_ref, wu_ref, cw_ref, cb_ref, wd_ref, g2_ref, b2_ref, o_ref, a_ref = rest
    tm = o_ref.shape[0]
    rows = lambda group: jnp.concatenate([r[...] for r in group], axis=0)

    if mlstm:
        hsum = rows(acts[0]).astype(_F32) + rows(acts[1]).astype(_F32)
        parts = []
        for hd in range(ML_HEADS):
            hh = hsum[:, hd * ML_DV:(hd + 1) * ML_DV]
            mu = jnp.mean(hh, axis=-1, keepdims=True)
            hc = hh - mu
            var = jnp.mean(hc * hc, axis=-1, keepdims=True)
            parts.append(hc * lax.rsqrt(var + EPS))
        hn = jnp.concatenate(parts, axis=-1) * nw_ref[...]
        act = (rows(acts[2]).astype(_F32) * hn).astype(_BF)
    else:
        act = rows(acts[0])
    z1 = ALPHA * rows(x_refs) + m_ref[0, 2:3, :] * _dot(act, wo_ref[...])
    x1 = _layer_norm_rows(z1, g1_ref[...], b1_ref[...])

    shift, scale = m_ref[0, 3:4, :], m_ref[0, 4:5, :]
    if halo:
        h, _ = _modulated_ext(x1[0:halo], x1[halo:halo + tm], x1[halo + tm:], shift, scale, seq_len,
                              pl.program_id(0))
    else:
        h = (x1 * (1.0 + scale) + shift).astype(_BF)
    zero_rows = jnp.zeros((_HALO, _FF_CHUNK), _F32)

    def pre_conv(cols):
        u = _dot(h, wu_ref[:, cols])
        return u if halo else jnp.concatenate([zero_rows, u, zero_rows], axis=0)

    for j in range(D_FF // _FF_CHUNK):
        gcols = slice(j * _FF_CHUNK, (j + 1) * _FF_CHUNK)
        vcols = slice(D_FF + j * _FF_CHUNK, D_FF + (j + 1) * _FF_CHUNK)
        gate = _conv3(pre_conv(gcols), cw_ref[:, gcols], cb_ref[:, gcols], tm)
        val = _conv3(pre_conv(vcols), cw_ref[:, vcols], cb_ref[:, vcols], tm)
        a_ref[:, gcols] = (_silu(gate) * val).astype(_BF)
    z2 = ALPHA * x1[halo:halo + tm] + m_ref[0, 5:6, :] * _dot(a_ref[...], wd_ref[...])
    o_ref[...] = _layer_norm_rows(z2, g2_ref[...], b2_ref[...])


def _mix_ffn(acts, w_o_bf, x2d, mods, ln_g, ln_b, w_up_bf, conv_w, conv_b, w_down_bf, *,
             seq_len, seq_base, tm, norm_w=None):
    rows = x2d.shape[0]
    mlstm = norm_w is not None
    halo = 0 if tm == seq_len else _MIX_HALO
    tps = seq_len // tm if seq_base else None
    vec_spec = pl.BlockSpec((1, D_MODEL), lambda i: (0, 0))
    vec = lambda a: a.reshape(1, D_MODEL)
    in_specs, args = [], []
    for arr in list(acts) + [x2d]:
        if halo:
            in_specs += _halo_specs(rows, tm, halo)
            args += [arr, arr, arr]
        else:
            in_specs.append(pl.BlockSpec((tm, D_MODEL), lambda i: (i, 0)))
            args.append(arr)
    if mlstm:
        in_specs.append(vec_spec)
        args.append(vec(norm_w))
    in_specs += [_mod_spec(seq_base, tps), _resident((D_MODEL, D_MODEL)), vec_spec, vec_spec,
                 _resident((D_MODEL, 2 * D_FF)), _resident((3, 2 * D_FF)), _resident((1, 2 * D_FF)),
                 _resident((D_FF, D_MODEL)), vec_spec, vec_spec]
    args += [mods, w_o_bf, vec(ln_g[0]), vec(ln_b[0]), w_up_bf, conv_w, conv_b.reshape(1, 2 * D_FF), w_down_bf,
             vec(ln_g[1]), vec(ln_b[1])]
    return pl.pallas_call(
        functools.partial(_mix_ffn_kernel, seq_len=seq_len, mlstm=mlstm, halo=halo),
        grid=(rows // tm,),
        in_specs=in_specs,
        out_specs=pl.BlockSpec((tm, D_MODEL), lambda i: (i, 0)),
        out_shape=jax.ShapeDtypeStruct((rows, D_MODEL), _F32),
        scratch_shapes=[pltpu.VMEM((tm, D_FF), _BF)],
        compiler_params=_params(("arbitrary",)),
        name="ml_out_ffn" if mlstm else "da_out_ffn",
    )(*args)


def _mlproj_kernel(prev_ref, x_ref, next_ref, m_ref, w_ref, wgt_ref, cw_ref, cb_ref, bgt_ref,
                   q_ref, k_ref, v_ref, og_ref, gtt_ref, *, seq_len):
    tm = x_ref.shape[0]
    shift, scale = m_ref[0, 0:1, :], m_ref[0, 1:2, :]
    h, hm = _modulated_ext(prev_ref[...], x_ref[...], next_ref[...], shift, scale, seq_len, pl.program_id(0))
    ch = _FF_CHUNK
    v0, o0 = 2 * ML_QK, 2 * ML_QK + ML_V
    pace = None
    for i in range(2 * ML_QK // ch):
        lo = i * ch
        cols = slice(lo, lo + ch)
        u = _dot(h, w_ref[:, cols])
        if pace is not None:
            u = u + pace
        y = _silu(_conv3(u, cw_ref[:, cols], cb_ref[:, cols], tm))
        if lo < ML_QK:
            q_ref[:, cols] = y.astype(_BF)
        else:
            k_ref[:, lo - ML_QK:lo - ML_QK + ch] = (y * (ML_DK ** -0.5)).astype(_BF)
        vc = _dot(hm, w_ref[:, v0 + lo:v0 + lo + ch])
        oc = _dot(hm, w_ref[:, o0 + lo:o0 + lo + ch])
        v_ref[:, cols] = vc.astype(_BF)
        og_ref[:, cols] = _sigmoid(oc).astype(_BF)
        pace = _exact_zero_like(vc[0:1, :]) + _exact_zero_like(oc[0:1, :])
    wg_hi, wg_lo = _split2(wgt_ref[...])
    gtt_ref[...] = _dot_nt(wg_hi, hm) + _dot_nt(wg_lo, hm) + bgt_ref[...]


def _ml_proj(x2d, mods, w_main_bf, w_gate, conv_w, conv_b, b_gate, *, seq_len, seq_base, tm):
    rows = x2d.shape[0]
    tps = seq_len // tm if seq_base else None
    prev, main, nxt = _halo_specs(rows, tm, _HALO)
    n_main = 2 * ML_QK + 2 * ML_V
    full = lambda shape: pl.BlockSpec(shape, lambda i: (0,) * len(shape))
    row = lambda n: pl.BlockSpec((tm, n), lambda i: (i, 0))
    return pl.pallas_call(
        functools.partial(_mlproj_kernel, seq_len=seq_len),
        grid=(rows // tm,),
        in_specs=[prev, main, nxt, _mod_spec(seq_base, tps), full((D_MODEL, n_main)),
                  full((ML_GATES, D_MODEL)), full((3, 2 * ML_QK)), full((1, 2 * ML_QK)), full((ML_GATES, 1))],
        out_specs=[row(ML_QK), row(ML_QK), row(ML_V), row(ML_V),
                   pl.BlockSpec((ML_GATES, tm), lambda i: (0, i))],
        out_shape=[jax.ShapeDtypeStruct((rows, ML_QK), _BF), jax.ShapeDtypeStruct((rows, ML_QK), _BF),
                   jax.ShapeDtypeStruct((rows, ML_V), _BF), jax.ShapeDtypeStruct((rows, ML_V), _BF),
                   jax.ShapeDtypeStruct((ML_GATES, rows), _F32)],
        compiler_params=_params(("arbitrary",)),
        name="ml_inproj_conv",
    )(x2d, x2d, x2d, mods, w_main_bf, w_gate.T, conv_w, conv_b.reshape(1, 2 * ML_QK),
      b_gate.reshape(ML_GATES, 1))


def _mlstm_weights(q, k, i_row, b_col, b_row, m_prev, n_prev, causal):
    t = q.shape[0]
    wide = lambda col: jnp.broadcast_to(col, (t, 128))
    tile2 = lambda w: jnp.concatenate([w] * (t // 128), axis=1)
    ti = lax.broadcasted_iota(jnp.int32, (t, t), 0)
    si = lax.broadcasted_iota(jnp.int32, (t, t), 1)
    a_row = i_row - b_row
    amat = jnp.where((si <= ti) if causal else (si >= ti), a_row, _NEG)
    g_col = jnp.maximum(m_prev, jnp.max(amat, axis=1, keepdims=True))
    g_w = wide(g_col)
    w_intra = jnp.exp(amat - tile2(g_w))
    w_inter_w = jnp.exp(m_prev - g_w)
    s = _dot_nt(q, k) * w_intra
    den_parts = w_inter_w * (q.astype(_F32) * n_prev)
    for c in range(t // 128):
        den_parts = den_parts + s[:, c * 128:(c + 1) * 128]
    den_col = jnp.sum(den_parts, axis=1, keepdims=True)
    r_col = 1.0 / jnp.maximum(jnp.abs(den_col), jnp.exp(-(b_col + g_col)))
    return s, w_inter_w, wide(r_col), a_row


def _mlstm_output(q, v, s_bf, w_inter_w, r_w, c_prev):
    num = jnp.concatenate([w_inter_w, w_inter_w], axis=1) * _dot(q, c_prev.astype(_BF)) + _dot(s_bf, v)
    return num * jnp.concatenate([r_w, r_w], axis=1)


def _mlstm_state(k, v, a_row, total, m_prev, c_prev, n_prev):
    t = k.shape[0]
    g_end = jnp.maximum(m_prev, jnp.max(a_row, axis=1, keepdims=True))
    w_s = jnp.exp(a_row - g_end)
    carry = jnp.exp(m_prev - g_end)
    kw_t = (k.astype(_F32).T * w_s).astype(_BF)
    c_new = carry * c_prev + _dot(kw_t, v)
    ws_hi, ws_lo = _split2(jnp.broadcast_to(w_s, (8, t)))
    n_new = carry * n_prev + (_dot(ws_hi, k) + _dot(ws_lo, k))[0:1, :]
    return c_new, n_new, total + g_end


def _mlstm_kernel(*refs, has_init, want_state):
    refs = list(refs)
    (qf_ref, kf_ref, vf_ref, gtf_ref, qb_ref, kb_ref, vb_ref, gtb_ref) = refs[:8]
    pos = 8
    if has_init:
        c0_ref, n0_ref, m0_ref = refs[pos:pos + 3]
        pos += 3
    hf_ref, hb_ref = refs[pos:pos + 2]
    pos += 2
    if want_state:
        co_ref, no_ref, mo_ref = refs[pos:pos + 3]
        pos += 3
    c_ref, n_ref, m_ref, s_ref, wi_ref, r_ref = refs[pos:pos + 6]
    ci = pl.program_id(1)
    t = qf_ref.shape[1]

    @pl.when(ci == 0)
    def _():
        if has_init:
            c_ref[...] = c0_ref[0]
            n_ref[...] = n0_ref[0]
            m_ref[...] = jnp.broadcast_to(m0_ref[0], m_ref.shape)
        else:
            c_ref[...] = jnp.zeros(c_ref.shape, _F32)
            n_ref[...] = jnp.zeros(n_ref.shape, _F32)
            m_ref[...] = jnp.zeros(m_ref.shape, _F32)

    ri = lax.broadcasted_iota(jnp.int32, (t, t), 0)
    cj = lax.broadcasted_iota(jnp.int32, (t, t), 1)
    lower = (cj <= ri).astype(_BF)
    upper = (cj >= ri).astype(_BF)

    def cum(mat, xt):
        return sum(_dot_nt(mat, p) for p in _split3(xt))

    def cum_t(xt, mat):
        return sum(_dot(p, mat) for p in _split3(xt))

    m_all, n_all = m_ref[...], n_ref[...]
    c_all = [c_ref[st] for st in range(2 * ML_HEADS)]
    dirs = ((qf_ref, kf_ref, vf_ref, gtf_ref, hf_ref), (qb_ref, kb_ref, vb_ref, gtb_ref, hb_ref))
    streams = []
    for d, (q_ref, k_ref, v_ref, gt_ref, h_ref) in enumerate(dirs):
        gt = gt_ref[...]
        is_f_row = (lax.broadcasted_iota(jnp.int32, (ML_GATES, 1), 0) % 8) >= 4
        xt = jnp.where(is_f_row, _log_sigmoid(gt), gt)
        if d == 0:
            bc = cum(lower, xt)
            br = cum_t(xt, upper)
        else:
            bc = cum(upper, xt)
            br = cum_t(xt, lower)
        tot = jnp.sum(xt, axis=1, keepdims=True)
        for hd in range(ML_HEADS):
            ic, fc = d * 8 + hd, d * 8 + 4 + hd
            st = d * ML_HEADS + hd
            q = q_ref[0, :, hd * ML_DK:(hd + 1) * ML_DK]
            k = k_ref[0, :, hd * ML_DK:(hd + 1) * ML_DK]
            s, w_inter_w, r_w, a_row = _mlstm_weights(
                q, k, xt[ic:ic + 1, :], bc[:, fc:fc + 1], br[fc:fc + 1, :], m_all[st:st + 1, 0:1],
                n_all[st:st + 1, :], d == 0)
            s_ref[st] = s.astype(_BF)
            wi_ref[st] = w_inter_w
            r_ref[st] = r_w
            streams.append((d, hd, st, a_row, tot[fc:fc + 1, :]))
    for d, hd, st, _, _ in streams:
        q_ref, _, v_ref, _, h_ref = dirs[d]
        h = _mlstm_output(q_ref[0, :, hd * ML_DK:(hd + 1) * ML_DK], v_ref[0, :, hd * ML_DV:(hd + 1) * ML_DV],
                          s_ref[st], wi_ref[st], r_ref[st], c_all[st])
        h_ref[0, :, hd * ML_DV:(hd + 1) * ML_DV] = h.astype(h_ref.dtype)
    for d, hd, st, a_row, total in streams:
        _, k_ref, v_ref, _, _ = dirs[d]
        c_new, n_new, m_new = _mlstm_state(
            k_ref[0, :, hd * ML_DK:(hd + 1) * ML_DK], v_ref[0, :, hd * ML_DV:(hd + 1) * ML_DV], a_row, total,
            m_all[st:st + 1, 0:1], c_all[st], n_all[st:st + 1, :])
        c_ref[st] = c_new
        n_ref[st:st + 1, :] = n_new
        m_ref[st:st + 1, :] = jnp.broadcast_to(m_new, (1, 128))

    if want_state:
        @pl.when(ci == pl.num_programs(1) - 1)
        def _():
            co_ref[0] = c_ref[...]
            no_ref[0] = n_ref[...]
            mo_ref[0] = m_ref[...]


def _mlstm_scan(q, k, v, gates_t, init, *, want_state):
    b, l, _ = q.shape
    t = _ML_CHUNK
    nc = l // t
    fwd = lambda n: pl.BlockSpec((1, t, n), lambda bi, ci: (bi, ci, 0))
    bwd = lambda n: pl.BlockSpec((1, t, n), lambda bi, ci: (bi, nc - 1 - ci, 0))
    gtf = pl.BlockSpec((ML_GATES, t), lambda bi, ci: (0, bi * nc + ci))
    gtb = pl.BlockSpec((ML_GATES, t), lambda bi, ci: (0, bi * nc + nc - 1 - ci))
    in_specs = [fwd(ML_QK), fwd(ML_QK), fwd(ML_V), gtf, bwd(ML_QK), bwd(ML_QK), bwd(ML_V), gtb]
    args = [q, k, v, gates_t, q, k, v, gates_t]
    has_init = init is not None
    if has_init:
        c0, n0, m0 = init
        in_specs += [pl.BlockSpec((1, 8, ML_DK, ML_DV), lambda bi, ci: (bi, 0, 0, 0)),
                     pl.BlockSpec((1, 8, ML_DK), lambda bi, ci: (bi, 0, 0)),
                     pl.BlockSpec((1, 8, 1), lambda bi, ci: (bi, 0, 0))]
        args += [c0.reshape(b, 8, ML_DK, ML_DV), n0.reshape(b, 8, ML_DK), m0.reshape(b, 8, 1)]
    out_specs = [fwd(ML_V), bwd(ML_V)]
    out_shape = [jax.ShapeDtypeStruct((b, l, ML_V), _BF), jax.ShapeDtypeStruct((b, l, ML_V), _BF)]
    if want_state:
        out_specs += [pl.BlockSpec((1, 8, ML_DK, ML_DV), lambda bi, ci: (bi, 0, 0, 0)),
                      pl.BlockSpec((1, 8, ML_DK), lambda bi, ci: (bi, 0, 0)),
                      pl.BlockSpec((1, 8, 128), lambda bi, ci: (bi, 0, 0))]
        out_shape += [jax.ShapeDtypeStruct((b, 8, ML_DK, ML_DV), _F32),
                      jax.ShapeDtypeStruct((b, 8, ML_DK), _F32),
                      jax.ShapeDtypeStruct((b, 8, 128), _F32)]
    return pl.pallas_call(
        functools.partial(_mlstm_kernel, has_init=has_init, want_state=want_state),
        grid=(b, nc),
        in_specs=in_specs,
        out_specs=out_specs,
        out_shape=out_shape,
        scratch_shapes=[pltpu.VMEM((8, ML_DK, ML_DV), _F32), pltpu.VMEM((8, ML_DK), _F32),
                        pltpu.VMEM((8, 128), _F32), pltpu.VMEM((8, t, t), _BF),
                        pltpu.VMEM((8, t, 128), _F32), pltpu.VMEM((8, t, 128), _F32)],
        compiler_params=_params(("arbitrary", "arbitrary")),
        name="mlstm_scan_state" if want_state else "mlstm_scan",
    )(*args)


def kernel(x_prompt, x_sample, c, cache_k, cache_v, state_C, state_n, state_m, c_ctx, ada_w, ada_b, ln_g, ln_b,
           da_w_qkv, da_lam, da_subln, da_w_o, ml_w_in, ml_conv_w, ml_conv_b, ml_b_gate, ml_norm_w, ml_w_out,
           ffn_w_up, ffn_conv_w, ffn_conv_b, ffn_w_down):
    bp, lp, _ = x_prompt.shape
    bs, ls, _ = x_sample.shape
    cond = jnp.concatenate([c_ctx[None, :], c, jnp.zeros((8 - 1 - bs, D_MODEL), _F32)], axis=0)
    mods = _modulation(cond, ada_w, ada_b).reshape(DEPTH, 8, 6, D_MODEL)

    xp = x_prompt.reshape(bp * lp, D_MODEL)
    xs = x_sample.reshape(bs * ls, D_MODEL)
    groups = (dict(seq_len=lp, seq_base=0), dict(seq_len=ls, seq_base=1))

    lam_init = 0.8 - 0.6 * math.exp(-0.3 * 0)
    w_qkv = da_w_qkv[0].astype(_BF)
    w_o = da_w_o[0].astype(_BF)
    qp, kp, vp = _qkv_proj(xp, mods[0], w_qkv, rope=False, kv_dtype=_F32, tm=_ROW_TILE, **groups[0])
    qs, ks, vs = _qkv_proj(xs, mods[0], w_qkv, rope=True, kv_dtype=_BF, tm=_ROW_TILE, **groups[1])
    as3 = lambda a, b: a.reshape(b, -1, D_MODEL)
    op = _self_attention(as3(qp, bp), as3(kp, bp), as3(vp, bp), da_lam[0], da_subln[0], lam_init=lam_init)
    os_ = _diff_attention_ctx(as3(qs, bs), cache_k[:, 0].reshape(bs, -1, D_MODEL),
                              cache_v[:, 0].reshape(bs, -1, D_MODEL), as3(ks, bs), as3(vs, bs),
                              da_lam[0], da_subln[0], lam_init=lam_init, tq=_ATTN_TQ, tk=_ATTN_TK)
    new_k = kp.reshape(bp, 1, lp, DA_HEADS, 2, DA_DK)
    new_v = vp.reshape(bp, 1, lp, DA_HEADS, DA_DV)

    def mix_ffn(acts, w_mix, x2d, i, grp, **kw):
        return _mix_ffn(acts, w_mix, x2d, mods[i], ln_g[i], ln_b[i], ffn_w_up[i].astype(_BF), ffn_conv_w[i],
                        ffn_conv_b[i], ffn_w_down[i].astype(_BF), tm=min(_ROW_TILE, grp["seq_len"]), **grp, **kw)

    xp = mix_ffn([op.reshape(-1, D_MODEL)], w_o, xp, 0, groups[0])
    xs = mix_ffn([os_.reshape(-1, D_MODEL)], w_o, xs, 0, groups[1])

    n_main = 2 * ML_QK + 2 * ML_V
    w_main = ml_w_in[0][:, :n_main].astype(_BF)
    w_gate = ml_w_in[0][:, n_main:]
    w_out = ml_w_out[0].astype(_BF)
    outs = []
    for x2d, grp, nb, init in ((xp, groups[0], bp, None),
                               (xs, groups[1], bs, (state_C[:, 0], state_n[:, 0], state_m[:, 0]))):
        q, k, v, og, gtt = _ml_proj(x2d, mods[1], w_main, w_gate, ml_conv_w[0], ml_conv_b[0], ml_b_gate[0],
                                        tm=min(_ROW_TILE, grp["seq_len"]), **grp)
        r3 = lambda a: a.reshape(nb, -1, a.shape[-1])
        res = _mlstm_scan(r3(q), r3(k), r3(v), gtt, init, want_state=init is None)
        hf, hb = res[0].reshape(-1, ML_V), res[1].reshape(-1, ML_V)
        outs.append((mix_ffn([hf, hb, og], w_out, x2d, 1, grp, norm_w=ml_norm_w[0]), res[2:]))
    (xp, (c_fin, n_fin, m_fin)), (xs, _) = outs
    new_c = c_fin.reshape(bp, 1, 2, ML_HEADS, ML_DK, ML_DV)
    new_n = n_fin.reshape(bp, 1, 2, ML_HEADS, ML_DK)
    new_m = m_fin[:, :, 0].reshape(bp, 1, 2, ML_HEADS)
    return (xp.reshape(bp, lp, D_MODEL), xs.reshape(bs, ls, D_MODEL), new_k, new_v, new_c, new_n, new_m)
```

```python
import functools
import math

import jax
import jax.numpy as jnp
from jax import lax
from jax.experimental import pallas as pl
from jax.experimental.pallas import tpu as pltpu

D_MODEL = 1024
DEPTH = 2
GRID_W = 64
DA_HEADS = 8
DA_DK = 64
DA_DV = 128
ML_HEADS = 4
ML_DK = 128
ML_DV = 256
ML_QK = ML_HEADS * ML_DK
ML_V = ML_HEADS * ML_DV
ML_GATES = 16
D_FF = 2816
ROPE_THETA = 10000.0
ALPHA = (2 * DEPTH) ** 0.25
EPS = 1e-5

_BF = jnp.bfloat16
_F32 = jnp.float32
_NEG = -1e30

_VMEM_LIMIT_BYTES = 56 * 1024 * 1024
_HALO = 8
_MIX_HALO = 16
_ML_CHUNK = 256
_ROW_TILE = 512
_FF_CHUNK = 256
_Q_SCALE = DA_DK ** -0.5 * math.log2(math.e)


def _dot(a, b):
    return jnp.dot(a, b, preferred_element_type=_F32)


def _dot_nt(a, b):
    return lax.dot_general(a, b, (((1,), (1,)), ((), ())), preferred_element_type=_F32)


def _split2(x):
    hi = x.astype(_BF)
    lo = (x - hi.astype(_F32)).astype(_BF)
    return hi, lo


def _split3(x):
    hi = x.astype(_BF)
    r = x - hi.astype(_F32)
    mid = r.astype(_BF)
    lo = (r - mid.astype(_F32)).astype(_BF)
    return hi, mid, lo


def _dot_f32(a, b):
    ah, al = _split2(a)
    bh, bl = _split2(b)
    return _dot(ah, bh) + _dot(al, bh) + _dot(ah, bl)


def _sigmoid(x):
    return 1.0 / (1.0 + jnp.exp(-x))


def _silu(x):
    return x * _sigmoid(x)


def _log_sigmoid(x):
    return jnp.minimum(x, 0.0) - jnp.log(1.0 + jnp.exp(-jnp.abs(x)))


def _layer_norm_rows(z, g, b):
    mu = jnp.mean(z, axis=-1, keepdims=True)
    zc = z - mu
    var = jnp.mean(zc * zc, axis=-1, keepdims=True)
    return zc * lax.rsqrt(var + EPS) * g + b


def _params(sem):
    return pltpu.CompilerParams(dimension_semantics=sem, vmem_limit_bytes=_VMEM_LIMIT_BYTES)


def _mod_kernel(c_ref, w_ref, b_ref, o_ref):
    s = _silu(c_ref[...])
    o_ref[0] = _dot_f32(s, w_ref[0]) + b_ref[0]


def _modulation(cond, ada_w, ada_b):
    tn = 1024
    n = 6 * D_MODEL
    return pl.pallas_call(
        _mod_kernel,
        grid=(DEPTH, n // tn),
        in_specs=[
            pl.BlockSpec((8, D_MODEL), lambda l, j: (0, 0)),
            pl.BlockSpec((1, D_MODEL, tn), lambda l, j: (l, 0, j)),
            pl.BlockSpec((1, 1, tn), lambda l, j: (l, 0, j)),
        ],
        out_specs=pl.BlockSpec((1, 8, tn), lambda l, j: (l, 0, j)),
        out_shape=jax.ShapeDtypeStruct((DEPTH, 8, n), _F32),
        compiler_params=_params(("arbitrary", "arbitrary")),
        name="adaln_mod",
    )(cond, ada_w, ada_b.reshape(DEPTH, 1, n))


def _mod_spec(seq_base, tiles_per_seq):
    if tiles_per_seq is None:
        return pl.BlockSpec((1, 6, D_MODEL), lambda i, *_: (seq_base, 0, 0))
    return pl.BlockSpec((1, 6, D_MODEL), lambda i, *_: (seq_base + i // tiles_per_seq, 0, 0))


def _qkv_kernel(*refs, rope):
    if rope:
        x_ref, m_ref, w_ref, cos_ref, sa_ref, sb_ref, q_ref, k_ref, v_ref = refs
    else:
        x_ref, m_ref, w_ref, q_ref, k_ref, v_ref = refs
    h = (x_ref[...] * (1.0 + m_ref[0, 1:2, :]) + m_ref[0, 0:1, :]).astype(_BF)
    for c, o_ref in enumerate((q_ref, k_ref, v_ref)):
        y = _dot(h, w_ref[:, c * D_MODEL:(c + 1) * D_MODEL])
        if rope and c < 2:
            cos, sa, sb = cos_ref[...], sa_ref[...], sb_ref[...]
            for hd in range(DA_HEADS):
                yh = y[:, hd * 128:(hd + 1) * 128]
                yh = yh * cos + pltpu.roll(yh, 112, 1) * sa + pltpu.roll(yh, 16, 1) * sb
                if c == 0:
                    yh = yh * _Q_SCALE
                o_ref[:, hd * 128:(hd + 1) * 128] = yh.astype(o_ref.dtype)
        else:
            if c == 0:
                y = y * _Q_SCALE
            o_ref[...] = y.astype(o_ref.dtype)


def _rope_tables(seq_len):
    t = jnp.arange(seq_len)
    row = (t // GRID_W).astype(_F32)
    col = (t % GRID_W).astype(_F32)
    lane = jnp.arange(128)
    d = lane % DA_DK
    half = DA_DK // 2
    nf = half // 2
    dd = d % half
    f = dd % nf
    odd = (dd // nf) == 1
    inv = ROPE_THETA ** (-jnp.arange(nf, dtype=_F32) / nf)
    pos = jnp.where((d < half)[None, :], row[:, None], col[:, None])
    ang = pos * inv[f][None, :]
    cos, sin = jnp.cos(ang), jnp.sin(ang)
    sa = jnp.where(odd[None, :], 0.0, -sin)
    sb = jnp.where(odd[None, :], sin, 0.0)
    return cos, sa, sb


def _qkv_proj(x2d, mods, w_bf, *, seq_len, seq_base, rope, kv_dtype, tm):
    rows = x2d.shape[0]
    tps = seq_len // tm if seq_base else None
    in_specs = [
        pl.BlockSpec((tm, D_MODEL), lambda i: (i, 0)),
        _mod_spec(seq_base, tps),
        pl.BlockSpec((D_MODEL, 3 * D_MODEL), lambda i: (0, 0)),
    ]
    args = [x2d, mods, w_bf]
    if rope:
        tabs = _rope_tables(seq_len)
        in_specs += [pl.BlockSpec((tm, 128), lambda i: (i % (seq_len // tm), 0))] * 3
        args += list(tabs)
    out_spec = pl.BlockSpec((tm, D_MODEL), lambda i: (i, 0))
    return pl.pallas_call(
        functools.partial(_qkv_kernel, rope=rope),
        grid=(rows // tm,),
        in_specs=in_specs,
        out_specs=[out_spec, out_spec, out_spec],
        out_shape=[jax.ShapeDtypeStruct((rows, D_MODEL), _BF),
                   jax.ShapeDtypeStruct((rows, D_MODEL), kv_dtype),
                   jax.ShapeDtypeStruct((rows, D_MODEL), kv_dtype)],
        compiler_params=_params(("arbitrary",)),
        name="da_qkv_rope" if rope else "da_qkv",
    )(*args)


_ATTN_TQ = 256
_ATTN_TK = 512


def _group_rows(x):
    return x.reshape(x.shape[0] // 8, 8, x.shape[1])


def _diff_lambda(lam_ref, lam_init):
    lf = lam_ref[...]
    return (jnp.exp(jnp.sum(lf[0:1] * lf[1:2], axis=1, keepdims=True))
            - jnp.exp(jnp.sum(lf[2:3] * lf[3:4], axis=1, keepdims=True)) + lam_init)


def _split_maps(q):
    first_map = lax.broadcasted_iota(jnp.int32, q.shape, 1) < DA_DK
    return jnp.where(first_map, q, jnp.zeros_like(q)), jnp.where(first_map, jnp.zeros_like(q), q)


def _diff_output(acc, lsum, lam, sub_ref, lam_init):
    l0 = jnp.sum(lsum[0], axis=0, keepdims=True)
    l1 = jnp.sum(lsum[1], axis=0, keepdims=True)
    o = acc[0] * (1.0 / l0) - acc[1] * (lam / l1)
    ms = jnp.mean(o * o, axis=0, keepdims=True)
    o = o * lax.rsqrt(ms + EPS) * (sub_ref[...] * (1.0 - lam_init))
    return o.T


def _attn_self_kernel(q_ref, k_ref, v_ref, lam_ref, sub_ref, o_ref, s_ref, *, lam_init):
    lam = _diff_lambda(lam_ref, lam_init)
    lanes = lambda hd: slice(hd * 128, (hd + 1) * 128)
    mx = []
    for hd in range(DA_HEADS):
        qm = _split_maps(q_ref[0, :, lanes(hd)])
        k = k_ref[0, :, lanes(hd)].astype(_BF)
        for mp in range(2):
            s = _dot_nt(k, qm[mp])
            s_ref[hd, mp] = s
            mx.append(jnp.max(jnp.max(_group_rows(s), axis=0), axis=0, keepdims=True))
    for hd in range(DA_HEADS):
        vt = v_ref[0, :, lanes(hd)].T.astype(_BF)
        acc, lsum = [], []
        for mp in range(2):
            e = jnp.exp2(s_ref[hd, mp] - mx[2 * hd + mp])
            acc.append(_dot(vt, e.astype(_BF)))
            lsum.append(jnp.sum(_group_rows(e), axis=0))
        o_ref[0, :, lanes(hd)] = _diff_output(acc, lsum, lam, sub_ref, lam_init).astype(o_ref.dtype)


def _self_attention(q, k, v, lam, subln, *, lam_init):
    b, l, _ = q.shape
    seq = pl.BlockSpec((1, l, D_MODEL), lambda bi: (bi, 0, 0))
    return pl.pallas_call(
        functools.partial(_attn_self_kernel, lam_init=lam_init),
        grid=(b,),
        in_specs=[seq, seq, seq, pl.BlockSpec((4, DA_DK), lambda bi: (0, 0)),
                  pl.BlockSpec((DA_DV, 1), lambda bi: (0, 0))],
        out_specs=seq,
        out_shape=jax.ShapeDtypeStruct((b, l, D_MODEL), _BF),
        scratch_shapes=[pltpu.VMEM((DA_HEADS, 2, l, l), _F32)],
        compiler_params=_params(("arbitrary",)),
        name="diff_attn",
    )(q, k, v, lam, subln.reshape(DA_DV, 1))


def _exact_zero_like(x):
    u = lax.bitcast_convert_type(x, jnp.uint32)
    u = lax.shift_right_logical(lax.shift_right_logical(u, jnp.uint32(16)), jnp.uint32(16))
    return lax.bitcast_convert_type(u, _F32)


def _attn_ctx_kernel(q_ref, ka_ref, kb_ref, va_ref, vb_ref, lam_ref, sub_ref, o_ref,
                     vta_ref, vtb_ref, s0_ref, s1_ref, mx0_ref, mx1_ref, *, nq, tk, lam_init):
    t = pl.program_id(0)
    la = ka_ref.shape[1]
    n_new = kb_ref.shape[1] // tk

    @pl.when(t == 0)
    def _():
        s1_ref[...] = jnp.zeros(s1_ref.shape, _F32)
        mx1_ref[...] = jnp.zeros(mx1_ref.shape, _F32)

    @pl.when(jnp.maximum(t - 1, 0) % nq == 0)
    def _():
        vta_ref[...] = va_ref[0].astype(_F32).T.astype(_BF)
        for j in range(n_new):
            vtb_ref[j] = vb_ref[0, j * tk:(j + 1) * tk, :].astype(_F32).T.astype(_BF)

    tiles = [(0, la, lambda: ka_ref[0].astype(_BF), lambda: vta_ref[...])]
    for j in range(n_new):
        tiles.append((la + j * tk, tk, lambda j=j: kb_ref[0, j * tk:(j + 1) * tk, :], lambda j=j: vtb_ref[j]))

    def step(s_w, mx_w, s_r, mx_r):
        qm = _split_maps(q_ref[0])
        mx = [jnp.max(mx_r[mp], axis=0, keepdims=True) for mp in range(2)]
        mrun, acc, lsum = [None, None], [None, None], [None, None]
        pace = None
        for off, rows, load_k, load_vt in tiles:
            kt = load_k()
            if pace is not None:
                kt = kt + pace
            for mp in range(2):
                s = _dot_nt(kt, qm[mp])
                s_w[mp, off:off + rows, :] = s
                m = jnp.max(_group_rows(s), axis=0)
                mrun[mp] = m if mrun[mp] is None else jnp.maximum(mrun[mp], m)
            for mp in range(2):
                e = jnp.exp2(s_r[mp, off:off + rows, :] - mx[mp])
                pv = _dot(load_vt(), e.astype(_BF))
                ls = jnp.sum(_group_rows(e), axis=0)
                acc[mp] = pv if acc[mp] is None else acc[mp] + pv
                lsum[mp] = ls if lsum[mp] is None else lsum[mp] + ls
            pace = _exact_zero_like(ls[0:1, 0:128]).astype(_BF)
        for mp in range(2):
            mx_w[mp] = mrun[mp]
        lam = _diff_lambda(lam_ref, lam_init)
        o_ref[0] = _diff_output(acc, lsum, lam, sub_ref, lam_init).astype(o_ref.dtype)

    @pl.when(t % 2 == 0)
    def _():
        step(s0_ref, mx0_ref, s1_ref, mx1_ref)

    @pl.when(t % 2 == 1)
    def _():
        step(s1_ref, mx1_ref, s0_ref, mx0_ref)


def _diff_attention_ctx(q, ka, va, kb, vb, lam, subln, *, lam_init, tq, tk):
    b, lq, _ = q.shape
    la, lb = ka.shape[1], kb.shape[1]
    nq = lq // tq
    units = b * DA_HEADS * nq

    def unit_index(u):
        bh = u // nq
        return bh // DA_HEADS, bh % DA_HEADS, u % nq

    def score_unit(t):
        return unit_index(jnp.minimum(t, units - 1))

    def value_unit(t):
        return unit_index(jnp.maximum(t - 1, 0))

    def q_map(t):
        bi, hi, qi = score_unit(t)
        return bi, qi, hi

    def k_map(t):
        bi, hi, _ = score_unit(t)
        return bi, 0, hi

    def v_map(t):
        bi, hi, _ = value_unit(t)
        return bi, 0, hi

    def o_map(t):
        bi, hi, qi = value_unit(t)
        return bi, qi, hi

    score_buf = pltpu.VMEM((2, la + lb, tq), _F32)
    max_buf = pltpu.VMEM((2, 8, tq), _F32)
    return pl.pallas_call(
        functools.partial(_attn_ctx_kernel, nq=nq, tk=tk, lam_init=lam_init),
        grid=(units + 1,),
        in_specs=[pl.BlockSpec((1, tq, 128), q_map),
                  pl.BlockSpec((1, la, 128), k_map), pl.BlockSpec((1, lb, 128), k_map),
                  pl.BlockSpec((1, la, 128), v_map), pl.BlockSpec((1, lb, 128), v_map),
                  pl.BlockSpec((4, DA_DK), lambda t: (0, 0)), pl.BlockSpec((DA_DV, 1), lambda t: (0, 0))],
        out_specs=pl.BlockSpec((1, tq, 128), o_map),
        out_shape=jax.ShapeDtypeStruct((b, lq, D_MODEL), _BF),
        scratch_shapes=[pltpu.VMEM((128, la), _BF), pltpu.VMEM((lb // tk, 128, tk), _BF),
                        score_buf, score_buf, max_buf, max_buf],
        compiler_params=_params(("arbitrary",)),
        name="diff_attn_ctx",
    )(q, ka, kb, va, vb, lam, subln.reshape(DA_DV, 1))


def _halo_specs(rows, tm, halo):
    assert tm % halo == 0 and rows % tm == 0
    per = tm // halo
    nblk = rows // halo
    main = pl.BlockSpec((tm, D_MODEL), lambda i: (i, 0))
    prev = pl.BlockSpec((halo, D_MODEL), lambda i: (jnp.maximum(i * per - 1, 0), 0))
    nxt = pl.BlockSpec((halo, D_MODEL), lambda i: (jnp.minimum((i + 1) * per, nblk - 1), 0))
    return [prev, main, nxt]


def _seq_keep_flags(tm, seq_len, tile):
    tiles = seq_len // tm
    t = tile % tiles
    return jnp.where(t != 0, 1.0, 0.0), jnp.where(t != tiles - 1, 1.0, 0.0)


def _modulated_ext(prev, main, nxt, shift, scale, seq_len, tile):
    keep_prev, keep_next = _seq_keep_flags(main.shape[0], seq_len, tile)
    mod = lambda r: r * (1.0 + scale) + shift
    main = mod(main)
    ext = jnp.concatenate([mod(prev) * keep_prev, main, mod(nxt) * keep_next], axis=0)
    return ext.astype(_BF), main.astype(_BF)


def _conv3(u, w, b, tm):
    rows = u.shape[0]
    halo = (rows - tm) // 2
    up = pltpu.roll(u, 1, 0)[halo:halo + tm]
    un = pltpu.roll(u, rows - 1, 0)[halo:halo + tm]
    return up * w[0:1, :] + u[halo:halo + tm] * w[1:2, :] + un * w[2:3, :] + b


def _resident(shape):
    return pl.BlockSpec(shape, lambda i: (0,) * len(shape), pipeline_mode=pl.Buffered(1))


def _mix_ffn_kernel(*refs, seq_len, mlstm, halo):
    refs = list(refs)
    per = 3 if halo else 1
    n_act = 3 if mlstm else 1
    acts = [refs[per * i:per * i + per] for i in range(n_act)]
    x_refs = refs[per * n_act:per * n_act + per]
    rest = refs[per * n_act + per:]
    if mlstm:
        nw_ref, rest = rest[0], rest[1:]
    m_ref, wo_ref, g1_ref, b1_ref, wu_ref, cw_ref, cb_ref, wd_ref, g2_ref, b2_ref, o_ref, a_ref = rest
    tm = o_ref.shape[0]
    rows = lambda group: jnp.concatenate([r[...] for r in group], axis=0)

    if mlstm:
        hsum = rows(acts[0]).astype(_F32) + rows(acts[1]).astype(_F32)
        parts = []
        for hd in range(ML_HEADS):
            hh = hsum[:, hd * ML_DV:(hd + 1) * ML_DV]
            mu = jnp.mean(hh, axis=-1, keepdims=True)
            hc = hh - mu
            var = jnp.mean(hc * hc, axis=-1, keepdims=True)
            parts.append(hc * lax.rsqrt(var + EPS))
        hn = jnp.concatenate(parts, axis=-1) * nw_ref[...]
        act = (rows(acts[2]).astype(_F32) * hn).astype(_BF)
    else:
        act = rows(acts[0])
    z1 = ALPHA * rows(x_refs) + m_ref[0, 2:3, :] * _dot(act, wo_ref[...])
    x1 = _layer_norm_rows(z1, g1_ref[...], b1_ref[...])

    shift, scale = m_ref[0, 3:4, :], m_ref[0, 4:5, :]
    if halo:
        h, _ = _modulated_ext(x1[0:halo], x1[halo:halo + tm], x1[halo + tm:], shift, scale, seq_len,
                              pl.program_id(0))
    else:
        h = (x1 * (1.0 + scale) + shift).astype(_BF)
    zero_rows = jnp.zeros((_HALO, _FF_CHUNK), _F32)

    def pre_conv(cols):
        u = _dot(h, wu_ref[:, cols])
        return u if halo else jnp.concatenate([zero_rows, u, zero_rows], axis=0)

    for j in range(D_FF // _FF_CHUNK):
        gcols = slice(j * _FF_CHUNK, (j + 1) * _FF_CHUNK)
        vcols = slice(D_FF + j * _FF_CHUNK, D_FF + (j + 1) * _FF_CHUNK)
        gate = _conv3(pre_conv(gcols), cw_ref[:, gcols], cb_ref[:, gcols], tm)
        val = _conv3(pre_conv(vcols), cw_ref[:, vcols], cb_ref[:, vcols], tm)
        a_ref[:, gcols] = (_silu(gate) * val).astype(_BF)
    z2 = ALPHA * x1[halo:halo + tm] + m_ref[0, 5:6, :] * _dot(a_ref[...], wd_ref[...])
    o_ref[...] = _layer_norm_rows(z2, g2_ref[...], b2_ref[...])


def _mix_ffn(acts, w_o_bf, x2d, mods, ln_g, ln_b, w_up_bf, conv_w, conv_b, w_down_bf, *,
             seq_len, seq_base, tm, norm_w=None):
    rows = x2d.shape[0]
    mlstm = norm_w is not None
    halo = 0 if tm == seq_len else _MIX_HALO
    tps = seq_len // tm if seq_base else None
    vec_spec = pl.BlockSpec((1, D_MODEL), lambda i: (0, 0))
    vec = lambda a: a.reshape(1, D_MODEL)
    in_specs, args = [], []
    for arr in list(acts) + [x2d]:
        if halo:
            in_specs += _halo_specs(rows, tm, halo)
            args += [arr, arr, arr]
        else:
            in_specs.append(pl.BlockSpec((tm, D_MODEL), lambda i: (i, 0)))
            args.append(arr)
    if mlstm:
        in_specs.append(vec_spec)
        args.append(vec(norm_w))
    in_specs += [_mod_spec(seq_base, tps), _resident((D_MODEL, D_MODEL)), vec_spec, vec_spec,
                 _resident((D_MODEL, 2 * D_FF)), _resident((3, 2 * D_FF)), _resident((1, 2 * D_FF)),
                 _resident((D_FF, D_MODEL)), vec_spec, vec_spec]
    args += [mods, w_o_bf, vec(ln_g[0]), vec(ln_b[0]), w_up_bf, conv_w, conv_b.reshape(1, 2 * D_FF), w_down_bf,
             vec(ln_g[1]), vec(ln_b[1])]
    return pl.pallas_call(
        functools.partial(_mix_ffn_kernel, seq_len=seq_len, mlstm=mlstm, halo=halo),
        grid=(rows // tm,),
        in_specs=in_specs,
        out_specs=pl.BlockSpec((tm, D_MODEL), lambda i: (i, 0)),
        out_shape=jax.ShapeDtypeStruct((rows, D_MODEL), _F32),
        scratch_shapes=[pltpu.VMEM((tm, D_FF), _BF)],
        compiler_params=_params(("arbitrary",)),
        name="ml_out_ffn" if mlstm else "da_out_ffn",
    )(*args)


def _mlproj_kernel(prev_ref, x_ref, next_ref, m_ref, w_ref, wgt_ref, cw_ref, cb_ref, bgt_ref,
                   q_ref, k_ref, v_ref, og_ref, gtt_ref, *, seq_len):
    tm = x_ref.shape[0]
    shift, scale = m_ref[0, 0:1, :], m_ref[0, 1:2, :]
    h, hm = _modulated_ext(prev_ref[...], x_ref[...], next_ref[...], shift, scale, seq_len, pl.program_id(0))
    ch = _FF_CHUNK
    v0, o0 = 2 * ML_QK, 2 * ML_QK + ML_V
    pace = None
    for i in range(2 * ML_QK // ch):
        lo = i * ch
        cols = slice(lo, lo + ch)
        u = _dot(h, w_ref[:, cols])
        if pace is not None:
            u = u + pace
        y = _silu(_conv3(u, cw_ref[:, cols], cb_ref[:, cols], tm))
        if lo < ML_QK:
            q_ref[:, cols] = y.astype(_BF)
        else:
            k_ref[:, lo - ML_QK:lo - ML_QK + ch] = (y * (ML_DK ** -0.5)).astype(_BF)
        vc = _dot(hm, w_ref[:, v0 + lo:v0 + lo + ch])
        oc = _dot(hm, w_ref[:, o0 + lo:o0 + lo + ch])
        v_ref[:, cols] = vc.astype(_BF)
        og_ref[:, cols] = _sigmoid(oc).astype(_BF)
        pace = _exact_zero_like(vc[0:1, :]) + _exact_zero_like(oc[0:1, :])
    wg_hi, wg_lo = _split2(wgt_ref[...])
    gtt_ref[...] = _dot_nt(wg_hi, hm) + _dot_nt(wg_lo, hm) + bgt_ref[...]


def _ml_proj(x2d, mods, w_main_bf, w_gate, conv_w, conv_b, b_gate, *, seq_len, seq_base, tm):
    rows = x2d.shape[0]
    tps = seq_len // tm if seq_base else None
    prev, main, nxt = _halo_specs(rows, tm, _HALO)
    n_main = 2 * ML_QK + 2 * ML_V
    full = lambda shape: pl.BlockSpec(shape, lambda i: (0,) * len(shape))
    row = lambda n: pl.BlockSpec((tm, n), lambda i: (i, 0))
    return pl.pallas_call(
        functools.partial(_mlproj_kernel, seq_len=seq_len),
        grid=(rows // tm,),
        in_specs=[prev, main, nxt, _mod_spec(seq_base, tps), full((D_MODEL, n_main)),
                  full((ML_GATES, D_MODEL)), full((3, 2 * ML_QK)), full((1, 2 * ML_QK)), full((ML_GATES, 1))],
        out_specs=[row(ML_QK), row(ML_QK), row(ML_V), row(ML_V),
                   pl.BlockSpec((ML_GATES, tm), lambda i: (0, i))],
        out_shape=[jax.ShapeDtypeStruct((rows, ML_QK), _BF), jax.ShapeDtypeStruct((rows, ML_QK), _BF),
                   jax.ShapeDtypeStruct((rows, ML_V), _BF), jax.ShapeDtypeStruct((rows, ML_V), _BF),
                   jax.ShapeDtypeStruct((ML_GATES, rows), _F32)],
        compiler_params=_params(("arbitrary",)),
        name="ml_inproj_conv",
    )(x2d, x2d, x2d, mods, w_main_bf, w_gate.T, conv_w, conv_b.reshape(1, 2 * ML_QK),
      b_gate.reshape(ML_GATES, 1))


def _mlstm_weights(q, k, i_row, b_col, b_row, m_prev, n_prev, causal):
    t = q.shape[0]
    wide = lambda col: jnp.broadcast_to(col, (t, 128))
    tile2 = lambda w: jnp.concatenate([w] * (t // 128), axis=1)
    ti = lax.broadcasted_iota(jnp.int32, (t, t), 0)
    si = lax.broadcasted_iota(jnp.int32, (t, t), 1)
    a_row = i_row - b_row
    amat = jnp.where((si <= ti) if causal else (si >= ti), a_row, _NEG)
    g_col = jnp.maximum(m_prev, jnp.max(amat, axis=1, keepdims=True))
    g_w = wide(g_col)
    w_intra = jnp.exp(amat - tile2(g_w))
    w_inter_w = jnp.exp(m_prev - g_w)
    s = _dot_nt(q, k) * w_intra
    den_parts = w_inter_w * (q.astype(_F32) * n_prev)
    for c in range(t // 128):
        den_parts = den_parts + s[:, c * 128:(c + 1) * 128]
    den_col = jnp.sum(den_parts, axis=1, keepdims=True)
    r_col = 1.0 / jnp.maximum(jnp.abs(den_col), jnp.exp(-(b_col + g_col)))
    return s, w_inter_w, wide(r_col), a_row


def _mlstm_output(q, v, s_bf, w_inter_w, r_w, c_prev):
    num = jnp.concatenate([w_inter_w, w_inter_w], axis=1) * _dot(q, c_prev.astype(_BF)) + _dot(s_bf, v)
    return num * jnp.concatenate([r_w, r_w], axis=1)


def _mlstm_state(k, v, a_row, total, m_prev, c_prev, n_prev):
    t = k.shape[0]
    g_end = jnp.maximum(m_prev, jnp.max(a_row, axis=1, keepdims=True))
    w_s = jnp.exp(a_row - g_end)
    carry = jnp.exp(m_prev - g_end)
    kw_t = (k.astype(_F32).T * w_s).astype(_BF)
    c_new = carry * c_prev + _dot(kw_t, v)
    ws_hi, ws_lo = _split2(jnp.broadcast_to(w_s, (8, t)))
    n_new = carry * n_prev + (_dot(ws_hi, k) + _dot(ws_lo, k))[0:1, :]
    return c_new, n_new, total + g_end


def _mlstm_kernel(*refs, has_init, want_state):
    refs = list(refs)
    (qf_ref, kf_ref, vf_ref, gtf_ref, qb_ref, kb_ref, vb_ref, gtb_ref) = refs[:8]
    pos = 8
    if has_init:
        c0_ref, n0_ref, m0_ref = refs[pos:pos + 3]
        pos += 3
    hf_ref, hb_ref = refs[pos:pos + 2]
    pos += 2
    if want_state:
        co_ref, no_ref, mo_ref = refs[pos:pos + 3]
        pos += 3
    c_ref, n_ref, m_ref, s_ref, wi_ref, r_ref = refs[pos:pos + 6]
    ci = pl.program_id(1)
    t = qf_ref.shape[1]

    @pl.when(ci == 0)
    def _():
        if has_init:
            c_ref[...] = c0_ref[0]
            n_ref[...] = n0_ref[0]
            m_ref[...] = jnp.broadcast_to(m0_ref[0], m_ref.shape)
        else:
            c_ref[...] = jnp.zeros(c_ref.shape, _F32)
            n_ref[...] = jnp.zeros(n_ref.shape, _F32)
            m_ref[...] = jnp.zeros(m_ref.shape, _F32)

    ri = lax.broadcasted_iota(jnp.int32, (t, t), 0)
    cj = lax.broadcasted_iota(jnp.int32, (t, t), 1)
    lower = (cj <= ri).astype(_BF)
    upper = (cj >= ri).astype(_BF)

    def cum(mat, xt):
        return sum(_dot_nt(mat, p) for p in _split3(xt))

    def cum_t(xt, mat):
        return sum(_dot(p, mat) for p in _split3(xt))

    m_all, n_all = m_ref[...], n_ref[...]
    c_all = [c_ref[st] for st in range(2 * ML_HEADS)]
    dirs = ((qf_ref, kf_ref, vf_ref, gtf_ref, hf_ref), (qb_ref, kb_ref, vb_ref, gtb_ref, hb_ref))
    streams = []
    for d, (q_ref, k_ref, v_ref, gt_ref, h_ref) in enumerate(dirs):
        gt = gt_ref[...]
        is_f_row = (lax.broadcasted_iota(jnp.int32, (ML_GATES, 1), 0) % 8) >= 4
        xt = jnp.where(is_f_row, _log_sigmoid(gt), gt)
        if d == 0:
            bc = cum(lower, xt)
            br = cum_t(xt, upper)
        else:
            bc = cum(upper, xt)
            br = cum_t(xt, lower)
        tot = jnp.sum(xt, axis=1, keepdims=True)
        for hd in range(ML_HEADS):
            ic, fc = d * 8 + hd, d * 8 + 4 + hd
            st = d * ML_HEADS + hd
            q = q_ref[0, :, hd * ML_DK:(hd + 1) * ML_DK]
            k = k_ref[0, :, hd * ML_DK:(hd + 1) * ML_DK]
            s, w_inter_w, r_w, a_row = _mlstm_weights(
                q, k, xt[ic:ic + 1, :], bc[:, fc:fc + 1], br[fc:fc + 1, :], m_all[st:st + 1, 0:1],
                n_all[st:st + 1, :], d == 0)
            s_ref[st] = s.astype(_BF)
            wi_ref[st] = w_inter_w
            r_ref[st] = r_w
            streams.append((d, hd, st, a_row, tot[fc:fc + 1, :]))
    for d, hd, st, _, _ in streams:
        q_ref, _, v_ref, _, h_ref = dirs[d]
        h = _mlstm_output(q_ref[0, :, hd * ML_DK:(hd + 1) * ML_DK], v_ref[0, :, hd * ML_DV:(hd + 1) * ML_DV],
                          s_ref[st], wi_ref[st], r_ref[st], c_all[st])
        h_ref[0, :, hd * ML_DV:(hd + 1) * ML_DV] = h.astype(h_ref.dtype)
    for d, hd, st, a_row, total in streams:
        _, k_ref, v_ref, _, _ = dirs[d]
        c_new, n_new, m_new = _mlstm_state(
            k_ref[0, :, hd * ML_DK:(hd + 1) * ML_DK], v_ref[0, :, hd * ML_DV:(hd + 1) * ML_DV], a_row, total,
            m_all[st:st + 1, 0:1], c_all[st], n_all[st:st + 1, :])
        c_ref[st] = c_new
        n_ref[st:st + 1, :] = n_new
        m_ref[st:st + 1, :] = jnp.broadcast_to(m_new, (1, 128))

    if want_state:
        @pl.when(ci == pl.num_programs(1) - 1)
        def _():
            co_ref[0] = c_ref[...]
            no_ref[0] = n_ref[...]
            mo_ref[0] = m_ref[...]


def _mlstm_scan(q, k, v, gates_t, init, *, want_state):
    b, l, _ = q.shape
    t = _ML_CHUNK
    nc = l // t
    fwd = lambda n: pl.BlockSpec((1, t, n), lambda bi, ci: (bi, ci, 0))
    bwd = lambda n: pl.BlockSpec((1, t, n), lambda bi, ci: (bi, nc - 1 - ci, 0))
    gtf = pl.BlockSpec((ML_GATES, t), lambda bi, ci: (0, bi * nc + ci))
    gtb = pl.BlockSpec((ML_GATES, t), lambda bi, ci: (0, bi * nc + nc - 1 - ci))
    in_specs = [fwd(ML_QK), fwd(ML_QK), fwd(ML_V), gtf, bwd(ML_QK), bwd(ML_QK), bwd(ML_V), gtb]
    args = [q, k, v, gates_t, q, k, v, gates_t]
    has_init = init is not None
    if has_init:
        c0, n0, m0 = init
        in_specs += [pl.BlockSpec((1, 8, ML_DK, ML_DV), lambda bi, ci: (bi, 0, 0, 0)),
                     pl.BlockSpec((1, 8, ML_DK), lambda bi, ci: (bi, 0, 0)),
                     pl.BlockSpec((1, 8, 1), lambda bi, ci: (bi, 0, 0))]
        args += [c0.reshape(b, 8, ML_DK, ML_DV), n0.reshape(b, 8, ML_DK), m0.reshape(b, 8, 1)]
    out_specs = [fwd(ML_V), bwd(ML_V)]
    out_shape = [jax.ShapeDtypeStruct((b, l, ML_V), _BF), jax.ShapeDtypeStruct((b, l, ML_V), _BF)]
    if want_state:
        out_specs += [pl.BlockSpec((1, 8, ML_DK, ML_DV), lambda bi, ci: (bi, 0, 0, 0)),
                      pl.BlockSpec((1, 8, ML_DK), lambda bi, ci: (bi, 0, 0)),
                      pl.BlockSpec((1, 8, 128), lambda bi, ci: (bi, 0, 0))]
        out_shape += [jax.ShapeDtypeStruct((b, 8, ML_DK, ML_DV), _F32),
                      jax.ShapeDtypeStruct((b, 8, ML_DK), _F32),
                      jax.ShapeDtypeStruct((b, 8, 128), _F32)]
    return pl.pallas_call(
        functools.partial(_mlstm_kernel, has_init=has_init, want_state=want_state),
        grid=(b, nc),
        in_specs=in_specs,
        out_specs=out_specs,
        out_shape=out_shape,
        scratch_shapes=[pltpu.VMEM((8, ML_DK, ML_DV), _F32), pltpu.VMEM((8, ML_DK), _F32),
                        pltpu.VMEM((8, 128), _F32), pltpu.VMEM((8, t, t), _BF),
                        pltpu.VMEM((8, t, 128), _F32), pltpu.VMEM((8, t, 128), _F32)],
        compiler_params=_params(("arbitrary", "arbitrary")),
        name="mlstm_scan_state" if want_state else "mlstm_scan",
    )(*args)


def kernel(x_prompt, x_sample, c, cache_k, cache_v, state_C, state_n, state_m, c_ctx, ada_w, ada_b, ln_g, ln_b,
           da_w_qkv, da_lam, da_subln, da_w_o, ml_w_in, ml_conv_w, ml_conv_b, ml_b_gate, ml_norm_w, ml_w_out,
           ffn_w_up, ffn_conv_w, ffn_conv_b, ffn_w_down):
    bp, lp, _ = x_prompt.shape
    bs, ls, _ = x_sample.shape
    cond = jnp.concatenate([c_ctx[None, :], c, jnp.zeros((8 - 1 - bs, D_MODEL), _F32)], axis=0)
    mods = _modulation(cond, ada_w, ada_b).reshape(DEPTH, 8, 6, D_MODEL)

    xp = x_prompt.reshape(bp * lp, D_MODEL)
    xs = x_sample.reshape(bs * ls, D_MODEL)
    groups = (dict(seq_len=lp, seq_base=0), dict(seq_len=ls, seq_base=1))

    lam_init = 0.8 - 0.6 * math.exp(-0.3 * 0)
    w_qkv = da_w_qkv[0].astype(_BF)
    w_o = da_w_o[0].astype(_BF)
    qp, kp, vp = _qkv_proj(xp, mods[0], w_qkv, rope=False, kv_dtype=_F32, tm=_ROW_TILE, **groups[0])
    qs, ks, vs = _qkv_proj(xs, mods[0], w_qkv, rope=True, kv_dtype=_BF, tm=_ROW_TILE, **groups[1])
    as3 = lambda a, b: a.reshape(b, -1, D_MODEL)
    op = _self_attention(as3(qp, bp), as3(kp, bp), as3(vp, bp), da_lam[0], da_subln[0], lam_init=lam_init)
    os_ = _diff_attention_ctx(as3(qs, bs), cache_k[:, 0].reshape(bs, -1, D_MODEL),
                              cache_v[:, 0].reshape(bs, -1, D_MODEL), as3(ks, bs), as3(vs, bs),
                              da_lam[0], da_subln[0], lam_init=lam_init, tq=_ATTN_TQ, tk=_ATTN_TK)
    new_k = kp.reshape(bp, 1, lp, DA_HEADS, 2, DA_DK)
    new_v = vp.reshape(bp, 1, lp, DA_HEADS, DA_DV)

    def mix_ffn(acts, w_mix, x2d, i, grp, **kw):
        return _mix_ffn(acts, w_mix, x2d, mods[i], ln_g[i], ln_b[i], ffn_w_up[i].astype(_BF), ffn_conv_w[i],
                        ffn_conv_b[i], ffn_w_down[i].astype(_BF), tm=min(_ROW_TILE, grp["seq_len"]), **grp, **kw)

    xp = mix_ffn([op.reshape(-1, D_MODEL)], w_o, xp, 0, groups[0])
    xs = mix_ffn([os_.reshape(-1, D_MODEL)], w_o, xs, 0, groups[1])

    n_main = 2 * ML_QK + 2 * ML_V
    w_main = ml_w_in[0][:, :n_main].astype(_BF)
    w_gate = ml_w_in[0][:, n_main:]
    w_out = ml_w_out[0].astype(_BF)
    outs = []
    for x2d, grp, nb, init in ((xp, groups[0], bp, None),
                               (xs, groups[1], bs, (state_C[:, 0], state_n[:, 0], state_m[:, 0]))):
        q, k, v, og, gtt = _ml_proj(x2d, mods[1], w_main, w_gate, ml_conv_w[0], ml_conv_b[0], ml_b_gate[0],
                                        tm=min(_ROW_TILE, grp["seq_len"]), **grp)
        r3 = lambda a: a.reshape(nb, -1, a.shape[-1])
        res = _mlstm_scan(r3(q), r3(k), r3(v), gtt, init, want_state=init is None)
        hf, hb = res[0].reshape(-1, ML_V), res[1].reshape(-1, ML_V)
        outs.append((mix_ffn([hf, hb, og], w_out, x2d, 1, grp, norm_w=ml_norm_w[0]), res[2:]))
    (xp, (c_fin, n_fin, m_fin)), (xs, _) = outs
    new_c = c_fin.reshape(bp, 1, 2, ML_HEADS, ML_DK, ML_DV)
    new_n = n_fin.reshape(bp, 1, 2, ML_HEADS, ML_DK)
    new_m = m_fin[:, :, 0].reshape(bp, 1, 2, ML_HEADS)
    return (xp.reshape(bp, lp, D_MODEL), xs.reshape(bs, ls, D_MODEL), new_k, new_v, new_c, new_n, new_m)
```
